```python
import jax, jax.numpy as jnp
from jax import lax
import numpy as np

D_MODEL = 1024
BATCH = 8
SEQ = 8192
DEPTH = 2

N_MIXERS = 2
MLA_HEADS = 8
MLA_Q_LORA = 256
MLA_KV_LORA = 256
MLA_NOPE = 128
MLA_ROPE = 64
MLA_V = 128
ROPE_THETA = 10000.0
Q_BLOCK = 128
HGRN_HEADS = 8
HGRN_DK = D_MODEL // HGRN_HEADS
HGRN_DV = D_MODEL // HGRN_HEADS
HGRN_CHUNK = 64
D_FF = ((8 * D_MODEL // 3 + 255) // 256) * 256
D_PLE = 256
LN_EPS = 1e-5
RMS_EPS = 1e-6
DEEPNORM_ALPHA = (2 * DEPTH) ** 0.25
DEEPNORM_BETA = (8 * DEPTH) ** -0.25
N_MLA_LAYERS = (DEPTH + N_MIXERS - 1) // N_MIXERS
N_HGRN_LAYERS = DEPTH // N_MIXERS

kernel_name = 'hybrid_mla_hgrn2_deepnorm_ple'


def layer_norm(x, g, b):
    xf = x.astype(jnp.float32)
    mu = jnp.mean(xf, axis=-1, keepdims=True)
    var = jnp.mean(jnp.square(xf - mu), axis=-1, keepdims=True)
    y = (xf - mu) * lax.rsqrt(var + LN_EPS) * g.astype(jnp.float32) + b.astype(jnp.float32)
    return y.astype(x.dtype)


def rms_norm(x, g):
    xf = x.astype(jnp.float32)
    y = xf * lax.rsqrt(jnp.mean(jnp.square(xf), axis=-1, keepdims=True) + RMS_EPS) * g.astype(jnp.float32)
    return y.astype(x.dtype)


def rope_tables(positions):
    inv_freq = ROPE_THETA ** (-jnp.arange(0, MLA_ROPE, 2, dtype=jnp.float32) / MLA_ROPE)
    ang = positions.astype(jnp.float32)[..., None] * inv_freq
    return jnp.cos(ang), jnp.sin(ang)


def apply_rope(t, cos, sin):
    half = t.shape[-1] // 2
    t1, t2 = t[..., :half], t[..., half:]
    out = jnp.concatenate([t1 * cos - t2 * sin, t2 * cos + t1 * sin], axis=-1)
    return out.astype(t.dtype)


def mla_mixer(x, cos, sin, w_dqkv, q_norm, kv_norm, w_uq, w_ukv, w_o):
    B, S, _ = x.shape
    H = MLA_HEADS
    down = x @ w_dqkv
    c_q = rms_norm(down[..., :MLA_Q_LORA], q_norm)
    c_kv = rms_norm(down[..., MLA_Q_LORA:MLA_Q_LORA + MLA_KV_LORA], kv_norm)
    k_rope = apply_rope(down[..., MLA_Q_LORA + MLA_KV_LORA:], cos, sin)
    q = (c_q @ w_uq).reshape(B, S, H, MLA_NOPE + MLA_ROPE)
    q_nope = q[..., :MLA_NOPE]
    q_rope = apply_rope(q[..., MLA_NOPE:], cos[:, :, None, :], sin[:, :, None, :])
    kv = (c_kv @ w_ukv).reshape(B, S, H, MLA_NOPE + MLA_V)
    k_nope, v = kv[..., :MLA_NOPE], kv[..., MLA_NOPE:]

    nblk = S // Q_BLOCK
    scale = (MLA_NOPE + MLA_ROPE) ** -0.5
    kpos = jnp.arange(S, dtype=jnp.int32)
    qpos = kpos.reshape(nblk, Q_BLOCK)

    def to_blocks(t):
        return jnp.moveaxis(t.reshape((B, nblk, Q_BLOCK) + t.shape[2:]), 1, 0)

    def attend(blk):
        qn, qr, qp = blk
        s = (jnp.einsum('bqhd,bkhd->bhqk', qn, k_nope)
             + jnp.einsum('bqhr,bkr->bhqk', qr, k_rope)).astype(jnp.float32) * scale
        s = jnp.where(kpos[None, None, None, :] <= qp[None, None, :, None], s, -jnp.inf)
        w = jax.nn.softmax(s, axis=-1).astype(v.dtype)
        return jnp.einsum('bhqk,bkhd->bqhd', w, v)

    o = lax.map(attend, (to_blocks(q_nope), to_blocks(q_rope), qpos))
    o = jnp.moveaxis(o, 0, 1).reshape(B, S, H * MLA_V)
    return o @ w_o


def hgrn2_mixer(x, lb, w_in, out_norm, w_o):
    B, S, _ = x.shape
    H, DK, DV, C = HGRN_HEADS, HGRN_DK, HGRN_DV, HGRN_CHUNK
    proj = x @ w_in
    q = jax.nn.silu(proj[..., :H * DK].astype(jnp.float32)).reshape(B, S, H, DK)
    f = proj[..., H * DK:2 * H * DK].astype(jnp.float32).reshape(B, S, H, DK)
    v = proj[..., 2 * H * DK:2 * H * DK + H * DV].astype(jnp.float32).reshape(B, S, H, DV)
    gate = proj[..., 2 * H * DK + H * DV:]
    lb = lb.reshape(H, DK)
    log_f = jnp.logaddexp(jnp.log(lb), jnp.log1p(-lb) + jax.nn.log_sigmoid(f))
    k = (1.0 - lb) * jax.nn.sigmoid(-f)

    n = S // C

    def to_chunks(t):
        return t.reshape(B, n, C, H, t.shape[-1]).transpose(1, 0, 3, 2, 4)

    causal = jnp.tril(jnp.ones((C, C), dtype=bool))

    def step(state, chunk):
        qc, kc, gc, vc = chunk
        G = jnp.cumsum(gc, axis=2)
        inter = jnp.einsum('bhtd,bhde->bhte', qc * jnp.exp(G), state)
        diff = G[:, :, :, None, :] - G[:, :, None, :, :]
        decay = jnp.exp(jnp.where(causal[:, :, None], diff, -jnp.inf))
        A = jnp.einsum('bhtd,bhsd,bhtsd->bhts', qc, kc, decay)
        intra = jnp.einsum('bhts,bhse->bhte', A, vc)
        G_last = G[:, :, -1:, :]
        new_state = (jnp.exp(G_last[:, :, 0, :])[..., None] * state
                     + jnp.einsum('bhsd,bhse->bhde', kc * jnp.exp(G_last - G), vc))
        return new_state, inter + intra

    s0 = jnp.zeros((B, H, DK, DV), jnp.float32)
    _, o = lax.scan(step, s0, (to_chunks(q), to_chunks(k), to_chunks(log_f), to_chunks(v)))
    o = o.transpose(1, 0, 3, 2, 4).reshape(B, S, H, DV)
    o = rms_norm(o, out_norm.reshape(H, DV))
    o = o * jax.nn.silu(gate.astype(jnp.float32)).reshape(B, S, H, DV)
    return o.reshape(B, S, H * DV).astype(x.dtype) @ w_o


def swiglu_ffn(x, w_in, w_down):
    h = x @ w_in
    g, u = h[..., :D_FF], h[..., D_FF:]
    return (jax.nn.silu(g) * u) @ w_down


def setup_inputs(seed: int = 0) -> dict:
    key = jax.random.key(seed)
    ks = jax.random.split(key, 24)

    def w(k, shape, fan_in, scale=1.0):
        return jax.random.normal(k, shape, jnp.float32) * (fan_in ** -0.5) * scale

    def gain(k, shape):
        return 1.0 + 0.05 * jax.random.normal(k, shape, jnp.float32)

    def bias(k, shape):
        return 0.02 * jax.random.normal(k, shape, jnp.float32)

    NM, NH = N_MLA_LAYERS, N_HGRN_LAYERS
    return {
        'x': jax.random.normal(ks[0], (BATCH, SEQ, D_MODEL), jnp.float32),
        'p': jax.random.normal(ks[1], (DEPTH, BATCH, SEQ, D_PLE), jnp.float32),
        'positions': jnp.tile(jnp.arange(SEQ, dtype=jnp.int32)[None, :], (BATCH, 1)),
        'mla_w_dqkv': w(ks[2], (NM, D_MODEL, MLA_Q_LORA + MLA_KV_LORA + MLA_ROPE), D_MODEL),
        'mla_q_norm': gain(ks[3], (NM, MLA_Q_LORA)),
        'mla_kv_norm': gain(ks[4], (NM, MLA_KV_LORA)),
        'mla_w_uq': w(ks[5], (NM, MLA_Q_LORA, MLA_HEADS * (MLA_NOPE + MLA_ROPE)), MLA_Q_LORA),
        'mla_w_ukv': w(ks[6], (NM, MLA_KV_LORA, MLA_HEADS * (MLA_NOPE + MLA_V)), MLA_KV_LORA),
        'mla_w_o': w(ks[7], (NM, MLA_HEADS * MLA_V, D_MODEL), MLA_HEADS * MLA_V, DEEPNORM_BETA),
        'hgrn_w_in': w(ks[8], (NH, D_MODEL, 2 * HGRN_HEADS * HGRN_DK + 2 * HGRN_HEADS * HGRN_DV), D_MODEL),
        'hgrn_lb_logits': 0.1 * jax.random.normal(ks[9], (DEPTH, HGRN_HEADS * HGRN_DK), jnp.float32),
        'hgrn_out_norm': gain(ks[10], (NH, HGRN_HEADS * HGRN_DV)),
        'hgrn_w_o': w(ks[11], (NH, HGRN_HEADS * HGRN_DV, D_MODEL), HGRN_HEADS * HGRN_DV, DEEPNORM_BETA),
        'ffn_w_in': w(ks[12], (DEPTH, D_MODEL, 2 * D_FF), D_MODEL),
        'ffn_w_down': w(ks[13], (DEPTH, D_FF, D_MODEL), D_FF, DEEPNORM_BETA),
        'ln_mix_g': gain(ks[14], (DEPTH, D_MODEL)),
        'ln_mix_b': bias(ks[15], (DEPTH, D_MODEL)),
        'ln_ffn_g': gain(ks[16], (DEPTH, D_MODEL)),
        'ln_ffn_b': bias(ks[17], (DEPTH, D_MODEL)),
        'ple_w_proj': w(ks[18], (DEPTH, D_PLE, D_MODEL), D_PLE),
        'ple_w_gate': w(ks[19], (DEPTH, D_MODEL, D_MODEL), D_MODEL),
    }


def reference(x, p, positions, mla_w_dqkv, mla_q_norm, mla_kv_norm, mla_w_uq, mla_w_ukv, mla_w_o,
              hgrn_w_in, hgrn_lb_logits, hgrn_out_norm, hgrn_w_o, ffn_w_in, ffn_w_down,
              ln_mix_g, ln_mix_b, ln_ffn_g, ln_ffn_b, ple_w_proj, ple_w_gate):
    cos, sin = rope_tables(positions)
    lb_soft = jax.nn.softmax(hgrn_lb_logits.astype(jnp.float32), axis=0)
    lower_bounds = jnp.cumsum(lb_soft, axis=0) - lb_soft[0:1]

    for i in range(DEPTH):
        j = i // N_MIXERS
        if i % N_MIXERS == 0:
            h = mla_mixer(x, cos, sin, mla_w_dqkv[j], mla_q_norm[j], mla_kv_norm[j],
                          mla_w_uq[j], mla_w_ukv[j], mla_w_o[j])
        else:
            h = hgrn2_mixer(x, lower_bounds[i], hgrn_w_in[j], hgrn_out_norm[j], hgrn_w_o[j])
        x = layer_norm(DEEPNORM_ALPHA * x + h, ln_mix_g[i], ln_mix_b[i])
        x = layer_norm(DEEPNORM_ALPHA * x + swiglu_ffn(x, ffn_w_in[i], ffn_w_down[i]), ln_ffn_g[i], ln_ffn_b[i])
        x = x + jax.nn.sigmoid(x @ ple_w_gate[i]) * (p[i] @ ple_w_proj[i])
    return x
```

```python
import functools

import jax
import jax.numpy as jnp
from jax import lax
from jax.experimental import pallas as pl
from jax.experimental.pallas import tpu as pltpu

F32 = jnp.float32
BF16 = jnp.bfloat16

D_MODEL = 1024
DEPTH = 2
N_MIXERS = 2
MLA_HEADS = 8
MLA_Q_LORA = 256
MLA_KV_LORA = 256
MLA_NOPE = 128
MLA_ROPE = 64
MLA_V = 128
ROPE_THETA = 10000.0
HGRN_HEADS = 8
HGRN_DK = D_MODEL // HGRN_HEADS
HGRN_DV = D_MODEL // HGRN_HEADS
D_FF = 2816
D_PLE = 256
LN_EPS = 1e-5
RMS_EPS = 1e-6
DEEPNORM_ALPHA = (2 * DEPTH) ** 0.25

LANES = 128
SUBLANES = 8
VMEM_LIMIT_BYTES = 56 * 1024 * 1024

QK_WIDTH = 2 * LANES
NEG_BIG = -1e30

TOKEN_TILE = 512
ATTN_BLOCK = 512
HGRN_SEQ_TILE = 512
HGRN_CHUNK = 128
FFN_CHUNK = 256


def _cparams(semantics):
    return pltpu.CompilerParams(dimension_semantics=semantics,
                                vmem_limit_bytes=VMEM_LIMIT_BYTES)


def _const_spec(shape):
    nd = len(shape)
    return pl.BlockSpec(shape, lambda *_: (0,) * nd)


def _sigmoid(x):
    return 1.0 / (1.0 + jnp.exp(-x))


def _layer_norm(y, g, b):
    mu = jnp.mean(y, axis=-1, keepdims=True)
    d = y - mu
    var = jnp.mean(d * d, axis=-1, keepdims=True)
    return d * lax.rsqrt(var + LN_EPS) * g + b


def _rms_norm(t, g):
    return t * lax.rsqrt(jnp.mean(t * t, axis=-1, keepdims=True) + RMS_EPS) * g


def _dot(a, b):
    return jnp.dot(a, b, preferred_element_type=F32)


def _dot_nt(a, b):
    return lax.dot_general(a, b, (((1,), (1,)), ((), ())), preferred_element_type=F32)


def _mla_pre_kernel(x_ref, pos_ref, invf_ref, wd_ref, gq_ref, gkv_ref, wq_ref, wkv_ref,
                    q_out, k_out, v_out, *, scale):
    nh = MLA_HEADS
    xb = x_ref[...].astype(BF16)
    down = _dot(xb, wd_ref[...])
    c_q = _rms_norm(down[:, :MLA_Q_LORA], gq_ref[...]).astype(BF16)
    c_kv = _rms_norm(down[:, MLA_Q_LORA:MLA_Q_LORA + MLA_KV_LORA], gkv_ref[...]).astype(BF16)
    ang = pos_ref[...] * invf_ref[...]
    cos = jnp.cos(ang)
    sin = jnp.sin(ang)
    base = MLA_Q_LORA + MLA_KV_LORA
    kr = (down[:, base:base + LANES] * cos + down[:, base + LANES:base + 2 * LANES] * sin).astype(BF16)

    qall = _dot(c_q, wq_ref[...])
    kvall = _dot(c_kv, wkv_ref[...])
    for h in range(nh):
        lo = h * LANES
        qn = qall[:, lo:lo + LANES]
        qr = qall[:, nh * LANES + lo:nh * LANES + lo + LANES]
        qx = qall[:, 2 * nh * LANES + lo:2 * nh * LANES + lo + LANES]
        q_out[:, h * QK_WIDTH:h * QK_WIDTH + LANES] = (qn * scale).astype(BF16)
        q_out[:, h * QK_WIDTH + LANES:(h + 1) * QK_WIDTH] = ((qr * cos + qx * sin) * scale).astype(BF16)
        k_out[:, h * QK_WIDTH:h * QK_WIDTH + LANES] = kvall[:, lo:lo + LANES].astype(BF16)
        k_out[:, h * QK_WIDTH + LANES:(h + 1) * QK_WIDTH] = kr
    v_out[...] = kvall[:, nh * LANES:].astype(BF16)


def _rope_slot(w):
    half = MLA_ROPE // 2
    t1, t2 = w[..., :half], w[..., half:]
    z = jnp.zeros(w.shape[:-1] + (LANES - MLA_ROPE,), w.dtype)
    return jnp.concatenate([t1, t2, z], -1), jnp.concatenate([-t2, t1, z], -1)


def _mla_pre(x2, pos2, w_dqkv, q_norm, kv_norm, w_uq, w_ukv):
    T = x2.shape[0]
    tm = min(TOKEN_TILE, T)
    nh = MLA_HEADS
    base = MLA_Q_LORA + MLA_KV_LORA
    slot, rot = _rope_slot(w_dqkv[:, base:])
    wd = jnp.concatenate([w_dqkv[:, :base], slot, rot], axis=1).astype(BF16)
    wq3 = w_uq.reshape(MLA_Q_LORA, nh, MLA_NOPE + MLA_ROPE)
    qslot, qrot = _rope_slot(wq3[:, :, MLA_NOPE:])
    wq = jnp.concatenate([wq3[:, :, :MLA_NOPE].reshape(MLA_Q_LORA, nh * LANES),
                          qslot.reshape(MLA_Q_LORA, nh * LANES),
                          qrot.reshape(MLA_Q_LORA, nh * LANES)], axis=1).astype(BF16)
    wkv3 = w_ukv.reshape(MLA_KV_LORA, nh, MLA_NOPE + MLA_V)
    wkv = jnp.concatenate([wkv3[:, :, :MLA_NOPE].reshape(MLA_KV_LORA, nh * LANES),
                           wkv3[:, :, MLA_NOPE:].reshape(MLA_KV_LORA, nh * LANES)], axis=1).astype(BF16)
    inv_freq = ROPE_THETA ** (-jnp.arange(0, MLA_ROPE, 2, dtype=F32) / MLA_ROPE)
    invf = jnp.concatenate([inv_freq, inv_freq, jnp.zeros((LANES - MLA_ROPE,), F32)])[None, :]
    scale = (MLA_NOPE + MLA_ROPE) ** -0.5
    row = lambda i: (i, 0)
    return pl.pallas_call(
        functools.partial(_mla_pre_kernel, scale=scale),
        grid=(T // tm,),
        in_specs=[pl.BlockSpec((tm, D_MODEL), row),
                  pl.BlockSpec((tm, 1), row),
                  _const_spec(invf.shape), _const_spec(wd.shape),
                  _const_spec((1, MLA_Q_LORA)), _const_spec((1, MLA_KV_LORA)),
                  _const_spec(wq.shape), _const_spec(wkv.shape)],
        out_specs=[pl.BlockSpec((tm, nh * QK_WIDTH), row),
                   pl.BlockSpec((tm, nh * QK_WIDTH), row),
                   pl.BlockSpec((tm, nh * MLA_V), row)],
        out_shape=[jax.ShapeDtypeStruct((T, nh * QK_WIDTH), BF16),
                   jax.ShapeDtypeStruct((T, nh * QK_WIDTH), BF16),
                   jax.ShapeDtypeStruct((T, nh * MLA_V), BF16)],
        compiler_params=_cparams(("parallel",)),
        name="mla_pre",
    )(x2, pos2, invf, wd, q_norm[None, :], kv_norm[None, :], wq, wkv)


def _attn_kernel(q_ref, k_ref, v_ref, o_ref, m_sc, l_sc, acc_sc, *, blk):
    qi = pl.program_id(2)
    q = q_ref[0]
    m_sc[...] = jnp.full(m_sc.shape, NEG_BIG, F32)
    l_sc[...] = jnp.zeros(l_sc.shape, F32)
    acc_sc[...] = jnp.zeros(acc_sc.shape, F32)
    reps = blk // LANES

    def step(j, masked):
        start = pl.multiple_of(j * blk, blk)
        k = k_ref[0, pl.ds(start, blk), :]
        v = v_ref[0, pl.ds(start, blk), :]
        s = _dot_nt(q, k)
        if masked:
            r = lax.broadcasted_iota(jnp.int32, s.shape, 0)
            c = lax.broadcasted_iota(jnp.int32, s.shape, 1)
            s = jnp.where(c <= r, s, NEG_BIG)
        m_prev = m_sc[...]
        m_new = jnp.maximum(m_prev, jnp.max(s, axis=1, keepdims=True))
        alpha = jnp.exp(m_prev - m_new)
        p = jnp.exp(s - jnp.tile(m_new, (1, reps)))
        l_sc[...] = alpha * l_sc[...] + jnp.sum(p, axis=1, keepdims=True)
        acc_sc[...] = alpha * acc_sc[...] + _dot(p.astype(BF16), v)
        m_sc[...] = m_new

    def body(j, carry):
        step(j, False)
        return carry

    lax.fori_loop(0, qi, body, 0)
    step(qi, True)
    o_ref[0] = (acc_sc[...] / l_sc[...]).astype(o_ref.dtype)


def _flash_attn(q, k, v, B, S):
    nh = MLA_HEADS
    blk = min(ATTN_BLOCK, S)
    q3 = q.reshape(B, S, nh * QK_WIDTH)
    k3 = k.reshape(B, S, nh * QK_WIDTH)
    v3 = v.reshape(B, S, nh * MLA_V)
    out = pl.pallas_call(
        functools.partial(_attn_kernel, blk=blk),
        grid=(B, nh, S // blk),
        in_specs=[pl.BlockSpec((1, blk, QK_WIDTH), lambda b, h, i: (b, i, h)),
                  pl.BlockSpec((1, S, QK_WIDTH), lambda b, h, i: (b, 0, h)),
                  pl.BlockSpec((1, S, MLA_V), lambda b, h, i: (b, 0, h))],
        out_specs=pl.BlockSpec((1, blk, MLA_V), lambda b, h, i: (b, i, h)),
        out_shape=jax.ShapeDtypeStruct((B, S, nh * MLA_V), BF16),
        scratch_shapes=[pltpu.VMEM((blk, LANES), F32),
                        pltpu.VMEM((blk, LANES), F32),
                        pltpu.VMEM((blk, MLA_V), F32)],
        compiler_params=_cparams(("parallel", "parallel", "arbitrary")),
        name="flash_attn",
    )(q3, k3, v3)
    return out.reshape(B * S, nh * MLA_V)


def _proj_ln_kernel(o_ref, x_ref, w_ref, g_ref, b_ref, out_ref):
    h = _dot(o_ref[...], w_ref[...])
    out_ref[...] = _layer_norm(DEEPNORM_ALPHA * x_ref[...] + h, g_ref[...], b_ref[...])


def _proj_ln(o, x2, w_o, g, b):
    T = x2.shape[0]
    tm = min(TOKEN_TILE, T)
    row = lambda i: (i, 0)
    return pl.pallas_call(
        _proj_ln_kernel,
        grid=(T // tm,),
        in_specs=[pl.BlockSpec((tm, D_MODEL), row), pl.BlockSpec((tm, D_MODEL), row),
                  _const_spec((D_MODEL, D_MODEL)), _const_spec((1, D_MODEL)), _const_spec((1, D_MODEL))],
        out_specs=pl.BlockSpec((tm, D_MODEL), row),
        out_shape=jax.ShapeDtypeStruct((T, D_MODEL), F32),
        compiler_params=_cparams(("parallel",)),
        name="proj_ln",
    )(o, x2, w_o.astype(BF16), g[None, :], b[None, :])


def _ffn_kernel(x_ref, p_ref, win_ref, wdown_ref, g_ref, b_ref, wgate_ref, wproj_ref, out_ref):
    x = x_ref[...]
    xb = x.astype(BF16)
    acc = jnp.zeros(x.shape, F32)
    for c in range(D_FF // FFN_CHUNK):
        lo = c * FFN_CHUNK
        gate = _dot(xb, win_ref[:, lo:lo + FFN_CHUNK])
        up = _dot(xb, win_ref[:, D_FF + lo:D_FF + lo + FFN_CHUNK])
        act = (gate * _sigmoid(gate) * up).astype(BF16)
        acc = acc + _dot(act, wdown_ref[lo:lo + FFN_CHUNK, :])
    y = _layer_norm(DEEPNORM_ALPHA * x + acc, g_ref[...], b_ref[...])
    gate = _sigmoid(_dot(y.astype(BF16), wgate_ref[...]))
    emb = _dot(p_ref[...].astype(BF16), wproj_ref[...])
    out_ref[...] = y + gate * emb


def _ffn_ln_ple(x2, p2, w_in, w_down, g, b, w_gate, w_proj):
    T = x2.shape[0]
    tm = min(TOKEN_TILE, T)
    row = lambda i: (i, 0)
    return pl.pallas_call(
        _ffn_kernel,
        grid=(T // tm,),
        in_specs=[pl.BlockSpec((tm, D_MODEL), row), pl.BlockSpec((tm, D_PLE), row),
                  _const_spec((D_MODEL, 2 * D_FF)), _const_spec((D_FF, D_MODEL)),
                  _const_spec((1, D_MODEL)), _const_spec((1, D_MODEL)),
                  _const_spec((D_MODEL, D_MODEL)), _const_spec((D_PLE, D_MODEL))],
        out_specs=pl.BlockSpec((tm, D_MODEL), row),
        out_shape=jax.ShapeDtypeStruct((T, D_MODEL), F32),
        compiler_params=_cparams(("parallel",)),
        name="ffn_ln_ple",
    )(x2, p2, w_in.astype(BF16), w_down.astype(BF16), g[None, :], b[None, :],
      w_gate.astype(BF16), w_proj.astype(BF16))


def _hgrn_pre_kernel(x_ref, w_ref, lbl_ref, q_out, k_out, lf_out, v_out, sg_out, *, layer):
    n = HGRN_HEADS * HGRN_DK
    logits = lbl_ref[...]
    mx = jnp.max(logits, axis=0, keepdims=True)
    e = jnp.exp(logits - mx)
    soft = e / jnp.sum(e, axis=0, keepdims=True)
    lb = jnp.zeros((1, n), F32)
    for j in range(1, layer + 1):
        lb = lb + soft[j:j + 1, :]
    log_lb = jnp.log(lb)
    log_1m = jnp.log1p(-lb)

    proj = _dot(x_ref[...].astype(BF16), w_ref[...])
    qp = proj[:, :n]
    f = proj[:, n:2 * n]
    gp = proj[:, 3 * n:]
    q_out[...] = qp * _sigmoid(qp)
    ef = jnp.exp(-jnp.abs(f))
    r = 1.0 / (1.0 + ef)
    k_out[...] = (1.0 - lb) * jnp.where(f >= 0, ef * r, r)
    b = log_1m + (jnp.minimum(f, 0.0) - jnp.log1p(ef))
    hi = jnp.maximum(log_lb, b)
    lo = jnp.minimum(log_lb, b)
    lf_out[...] = hi + jnp.log1p(jnp.exp(lo - hi))
    v_out[...] = proj[:, 2 * n:3 * n]
    sg_out[...] = gp * _sigmoid(gp)


def _hgrn_pre(x2, w_in, lb_logits, layer):
    T = x2.shape[0]
    tm = min(TOKEN_TILE, T)
    n = HGRN_HEADS * HGRN_DK
    row = lambda i: (i, 0)
    outs = [jax.ShapeDtypeStruct((T, n), F32)] * 5
    return pl.pallas_call(
        functools.partial(_hgrn_pre_kernel, layer=layer),
        grid=(T // tm,),
        in_specs=[pl.BlockSpec((tm, D_MODEL), row), _const_spec((D_MODEL, 4 * n)),
                  _const_spec((DEPTH, n))],
        out_specs=[pl.BlockSpec((tm, n), row)] * 5,
        out_shape=outs,
        compiler_params=_cparams(("parallel",)),
        name="hgrn_pre",
    )(x2, w_in.astype(BF16), lb_logits)


def _split3(g):
    g1 = g.astype(BF16)
    r1 = g - g1.astype(F32)
    g2 = r1.astype(BF16)
    g3 = (r1 - g2.astype(F32)).astype(BF16)
    return g1, g2, g3


def _level_ref(g_sc, h, chunk):
    pieces = []
    for r in range(chunk // SUBLANES):
        r0 = r * SUBLANES

        def bc(row):
            return jnp.broadcast_to(g_sc[row:row + 1, :], (SUBLANES, LANES))

        if h >= SUBLANES // 2:
            mid = (r0 // (2 * h)) * (2 * h) + h - 1
            pieces.append(bc(mid))
        else:
            sub = lax.broadcasted_iota(jnp.int32, (SUBLANES, LANES), 0)
            piece = bc(r0 + h - 1)
            for blk_start in range(2 * h, SUBLANES, 2 * h):
                piece = jnp.where(sub >= blk_start, bc(r0 + blk_start + h - 1), piece)
            pieces.append(piece)
    return jnp.concatenate(pieces, axis=0)


def _hgrn_rec_kernel(q_ref, k_ref, lf_ref, v_ref, sg_ref, gn_ref, o_ref, st_sc, g_sc, *, chunk, n_chunks):
    @pl.when(pl.program_id(2) == 0)
    def _():
        st_sc[...] = jnp.zeros(st_sc.shape, F32)

    ti = lax.broadcasted_iota(jnp.int32, (chunk, chunk), 0)
    si = lax.broadcasted_iota(jnp.int32, (chunk, chunk), 1)
    tri = (si <= ti).astype(BF16)
    txs = jnp.bitwise_xor(ti, si)
    rowi = lax.broadcasted_iota(jnp.int32, (chunk, LANES), 0)
    gn = gn_ref[...]

    def chunk_body(c, carry):
        rows = pl.ds(pl.multiple_of(c * chunk, chunk), chunk)
        g = lf_ref[0, rows, :]
        q = q_ref[0, rows, :]
        k = k_ref[0, rows, :]
        v = v_ref[0, rows, :]
        g1, g2, g3 = _split3(g)
        G = _dot(tri, g1) + _dot(tri, g2) + _dot(tri, g3)
        g_sc[...] = G
        g_last = G[chunk - 1:chunk, :]

        st = st_sc[...]
        inter = _dot_nt((q * jnp.exp(G)).astype(BF16), st.astype(BF16))

        a = jnp.zeros((chunk, chunk), F32)
        h = chunk // 2
        while h >= 1:
            shift = h.bit_length() - 1
            if h == 1:
                d = jnp.where((rowi & 1) == 1, g, 0.0)
                arg = d
            else:
                d = G - _level_ref(g_sc, h, chunk)
                up = ((rowi >> shift) & 1) == 1
                arg = jnp.where(up, d, -d)
            e = jnp.exp(arg)
            al = _dot_nt((q * e).astype(BF16), (k * e).astype(BF16))
            a = jnp.where(((txs >> shift) == 1) & (ti > si), al, a)
            h //= 2
        a = jnp.where(ti == si, jnp.sum(q * k, axis=1, keepdims=True), a)
        o = inter + _dot(a.astype(BF16), v.astype(BF16))

        kdec = (k * jnp.exp(g_last - G)).astype(BF16)
        upd = _dot(v.T.astype(BF16), kdec)
        st_sc[...] = st * jnp.exp(g_last) + upd

        on = _rms_norm(o, gn)
        o_ref[0, rows, :] = (on * sg_ref[0, rows, :]).astype(o_ref.dtype)
        return carry

    lax.fori_loop(0, n_chunks, chunk_body, 0)


def _hgrn_rec(q, k, lf, v, sg, out_norm, B, S):
    nh = HGRN_HEADS
    ts = min(HGRN_SEQ_TILE, S)
    chunk = min(HGRN_CHUNK, ts)
    n = nh * HGRN_DK
    r3 = lambda t: t.reshape(B, S, n)
    blk = pl.BlockSpec((1, ts, LANES), lambda b, h, i: (b, i, h))
    out = pl.pallas_call(
        functools.partial(_hgrn_rec_kernel, chunk=chunk, n_chunks=ts // chunk),
        grid=(B, nh, S // ts),
        in_specs=[blk, blk, blk, blk, blk, pl.BlockSpec((1, LANES), lambda b, h, i: (0, h))],
        out_specs=blk,
        out_shape=jax.ShapeDtypeStruct((B, S, n), BF16),
        scratch_shapes=[pltpu.VMEM((HGRN_DV, HGRN_DK), F32),
                        pltpu.VMEM((chunk, LANES), F32)],
        compiler_params=_cparams(("parallel", "parallel", "arbitrary")),
        name="hgrn_rec",
    )(r3(q), r3(k), r3(lf), r3(v), r3(sg), out_norm[None, :])
    return out.reshape(B * S, n)


def kernel(x, p, positions, mla_w_dqkv, mla_q_norm, mla_kv_norm, mla_w_uq, mla_w_ukv, mla_w_o,
           hgrn_w_in, hgrn_lb_logits, hgrn_out_norm, hgrn_w_o, ffn_w_in, ffn_w_down,
           ln_mix_g, ln_mix_b, ln_ffn_g, ln_ffn_b, ple_w_proj, ple_w_gate):
    B, S, D = x.shape
    T = B * S
    x2 = x.reshape(T, D)
    pos2 = positions.astype(F32).reshape(T, 1)
    depth = p.shape[0]
    for i in range(depth):
        j = i // N_MIXERS
        if i % N_MIXERS == 0:
            q, k, v = _mla_pre(x2, pos2, mla_w_dqkv[j], mla_q_norm[j], mla_kv_norm[j],
                               mla_w_uq[j], mla_w_ukv[j])
            o = _flash_attn(q, k, v, B, S)
            w_o = mla_w_o[j]
        else:
            q, k, lf, v, sg = _hgrn_pre(x2, hgrn_w_in[j], hgrn_lb_logits, i)
            o = _hgrn_rec(q, k, lf, v, sg, hgrn_out_norm[j], B, S)
            w_o = hgrn_w_o[j]
        x2 = _proj_ln(o, x2, w_o, ln_mix_g[i], ln_mix_b[i])
        x2 = _ffn_ln_ple(x2, p[i].reshape(T, D_PLE), ffn_w_in[i], ffn_w_down[i],
                         ln_ffn_g[i], ln_ffn_b[i], ple_w_gate[i], ple_w_proj[i])
    return x2.reshape(B, S, D)
```

```python
import functools

import jax
import jax.numpy as jnp
from jax import lax
from jax.experimental import pallas as pl
from jax.experimental.pallas import tpu as pltpu

F32 = jnp.float32
BF16 = jnp.bfloat16

D_MODEL = 1024
DEPTH = 2
N_MIXERS = 2
MLA_HEADS = 8
MLA_Q_LORA = 256
MLA_KV_LORA = 256
MLA_NOPE = 128
MLA_ROPE = 64
MLA_V = 128
ROPE_THETA = 10000.0
HGRN_HEADS = 8
HGRN_DK = D_MODEL // HGRN_HEADS
HGRN_DV = D_MODEL // HGRN_HEADS
D_FF = 2816
D_PLE = 256
LN_EPS = 1e-5
RMS_EPS = 1e-6
DEEPNORM_ALPHA = (2 * DEPTH) ** 0.25

LANES = 128
SUBLANES = 8
VMEM_LIMIT_BYTES = 56 * 1024 * 1024

QK_WIDTH = 2 * LANES
NEG_BIG = -1e30

TOKEN_TILE = 512
ATTN_Q_TILE = 2048
ATTN_KV_SUB = 256
V_WIDTH = 2 * LANES
LOG2_E = 1.4426950408889634
HGRN_SEQ_TILE = 512
HGRN_CHUNK = 128
FFN_CHUNK = 256


def _cparams(semantics):
    return pltpu.CompilerParams(dimension_semantics=semantics,
                                vmem_limit_bytes=VMEM_LIMIT_BYTES)


def _const_spec(shape):
    nd = len(shape)
    return pl.BlockSpec(shape, lambda *_: (0,) * nd)


def _sigmoid(x):
    return 1.0 / (1.0 + jnp.exp(-x))


def _layer_norm(y, g, b):
    mu = jnp.mean(y, axis=-1, keepdims=True)
    d = y - mu
    var = jnp.mean(d * d, axis=-1, keepdims=True)
    return d * lax.rsqrt(var + LN_EPS) * g + b


def _rms_norm(t, g):
    return t * lax.rsqrt(jnp.mean(t * t, axis=-1, keepdims=True) + RMS_EPS) * g


def _dot(a, b):
    return jnp.dot(a, b, preferred_element_type=F32)


def _dot_nt(a, b):
    return lax.dot_general(a, b, (((1,), (1,)), ((), ())), preferred_element_type=F32)


def _mla_pre_kernel(x_ref, pos_ref, invf_ref, wd_ref, gq_ref, gkv_ref, wq_ref, wkv_ref,
                    q_out, k_out, v_out, *, scale):
    nh = MLA_HEADS
    xb = x_ref[...].astype(BF16)
    down = _dot(xb, wd_ref[...])
    c_q = _rms_norm(down[:, :MLA_Q_LORA], gq_ref[...]).astype(BF16)
    c_kv = _rms_norm(down[:, MLA_Q_LORA:MLA_Q_LORA + MLA_KV_LORA], gkv_ref[...]).astype(BF16)
    ang = pos_ref[...] * invf_ref[...]
    cos = jnp.cos(ang)
    sin = jnp.sin(ang)
    base = MLA_Q_LORA + MLA_KV_LORA
    kr = (down[:, base:base + LANES] * cos + down[:, base + LANES:base + 2 * LANES] * sin).astype(BF16)

    qall = _dot(c_q, wq_ref[...])
    kvall = _dot(c_kv, wkv_ref[...])
    for h in range(nh):
        lo = h * LANES
        qn = qall[:, lo:lo + LANES]
        qr = qall[:, nh * LANES + lo:nh * LANES + lo + LANES]
        qx = qall[:, 2 * nh * LANES + lo:2 * nh * LANES + lo + LANES]
        q_out[:, h * QK_WIDTH:h * QK_WIDTH + LANES] = (qn * scale).astype(BF16)
        q_out[:, h * QK_WIDTH + LANES:(h + 1) * QK_WIDTH] = ((qr * cos + qx * sin) * scale).astype(BF16)
        k_out[:, h * QK_WIDTH:h * QK_WIDTH + LANES] = kvall[:, lo:lo + LANES].astype(BF16)
        k_out[:, h * QK_WIDTH + LANES:(h + 1) * QK_WIDTH] = kr
        v_out[:, h * V_WIDTH:h * V_WIDTH + LANES] = kvall[:, nh * LANES + lo:nh * LANES + lo + LANES].astype(BF16)
        v_out[:, h * V_WIDTH + LANES:(h + 1) * V_WIDTH] = jnp.ones((x_ref.shape[0], LANES), BF16)


def _rope_slot(w):
    half = MLA_ROPE // 2
    t1, t2 = w[..., :half], w[..., half:]
    z = jnp.zeros(w.shape[:-1] + (LANES - MLA_ROPE,), w.dtype)
    return jnp.concatenate([t1, t2, z], -1), jnp.concatenate([-t2, t1, z], -1)


def _mla_pre(x2, pos2, w_dqkv, q_norm, kv_norm, w_uq, w_ukv):
    T = x2.shape[0]
    tm = min(TOKEN_TILE, T)
    nh = MLA_HEADS
    base = MLA_Q_LORA + MLA_KV_LORA
    slot, rot = _rope_slot(w_dqkv[:, base:])
    wd = jnp.concatenate([w_dqkv[:, :base], slot, rot], axis=1).astype(BF16)
    wq3 = w_uq.reshape(MLA_Q_LORA, nh, MLA_NOPE + MLA_ROPE)
    qslot, qrot = _rope_slot(wq3[:, :, MLA_NOPE:])
    wq = jnp.concatenate([wq3[:, :, :MLA_NOPE].reshape(MLA_Q_LORA, nh * LANES),
                          qslot.reshape(MLA_Q_LORA, nh * LANES),
                          qrot.reshape(MLA_Q_LORA, nh * LANES)], axis=1).astype(BF16)
    wkv3 = w_ukv.reshape(MLA_KV_LORA, nh, MLA_NOPE + MLA_V)
    wkv = jnp.concatenate([wkv3[:, :, :MLA_NOPE].reshape(MLA_KV_LORA, nh * LANES),
                           wkv3[:, :, MLA_NOPE:].reshape(MLA_KV_LORA, nh * LANES)], axis=1).astype(BF16)
    inv_freq = ROPE_THETA ** (-jnp.arange(0, MLA_ROPE, 2, dtype=F32) / MLA_ROPE)
    invf = jnp.concatenate([inv_freq, inv_freq, jnp.zeros((LANES - MLA_ROPE,), F32)])[None, :]
    scale = (MLA_NOPE + MLA_ROPE) ** -0.5 * LOG2_E
    row = lambda i: (i, 0)
    return pl.pallas_call(
        functools.partial(_mla_pre_kernel, scale=scale),
        grid=(T // tm,),
        in_specs=[pl.BlockSpec((tm, D_MODEL), row),
                  pl.BlockSpec((tm, 1), row),
                  _const_spec(invf.shape), _const_spec(wd.shape),
                  _const_spec((1, MLA_Q_LORA)), _const_spec((1, MLA_KV_LORA)),
                  _const_spec(wq.shape), _const_spec(wkv.shape)],
        out_specs=[pl.BlockSpec((tm, nh * QK_WIDTH), row),
                   pl.BlockSpec((tm, nh * QK_WIDTH), row),
                   pl.BlockSpec((tm, nh * V_WIDTH), row)],
        out_shape=[jax.ShapeDtypeStruct((T, nh * QK_WIDTH), BF16),
                   jax.ShapeDtypeStruct((T, nh * QK_WIDTH), BF16),
                   jax.ShapeDtypeStruct((T, nh * V_WIDTH), BF16)],
        compiler_params=_cparams(("parallel",)),
        name="mla_pre",
    )(x2, pos2, invf, wd, q_norm[None, :], kv_norm[None, :], wq, wkv)


def _attn_kernel(q_ref, k_ref, v_ref, o_ref, m_sc, acc_sc, *, tq, sub):
    qi = pl.program_id(2)
    m_sc[...] = jnp.full(m_sc.shape, NEG_BIG, F32)
    acc_sc[...] = jnp.zeros(acc_sc.shape, F32)
    nsub = tq // sub

    def sub_step(kv_start, row0, diag_col0):
        q = q_ref[0, row0:, :]
        k = k_ref[0, pl.ds(kv_start, sub), :]
        v = v_ref[0, pl.ds(kv_start, sub), :]
        s = _dot_nt(q, k)
        if diag_col0 is not None:
            r = lax.broadcasted_iota(jnp.int32, s.shape, 0) + row0
            c = lax.broadcasted_iota(jnp.int32, s.shape, 1) + diag_col0
            s = jnp.where(c <= r, s, NEG_BIG)
        m_prev = m_sc[row0:, :]
        m_new = jnp.maximum(m_prev, jnp.max(s, axis=1, keepdims=True))
        alpha = jnp.exp2(m_prev - m_new)
        p = jnp.exp2(s - jnp.tile(m_new, (1, sub // LANES))).astype(BF16)
        acc_sc[row0:, :] = jnp.tile(alpha, (1, 2)) * acc_sc[row0:, :] + _dot(p, v)
        m_sc[row0:, :] = m_new

    def body(j, carry):
        base = j * tq
        for c in range(nsub):
            sub_step(pl.multiple_of(base + c * sub, sub), 0, None)
        return carry

    lax.fori_loop(0, qi, body, 0)
    base = qi * tq
    for c in range(nsub):
        sub_step(pl.multiple_of(base + c * sub, sub), c * sub, c * sub)
    acc = acc_sc[...]
    o_ref[0] = (acc[:, :MLA_V] / acc[:, MLA_V:]).astype(o_ref.dtype)


def _flash_attn(q, k, v, B, S):
    nh = MLA_HEADS
    tq = min(ATTN_Q_TILE, S)
    sub = min(ATTN_KV_SUB, tq)
    q3 = q.reshape(B, S, nh * QK_WIDTH)
    k3 = k.reshape(B, S, nh * QK_WIDTH)
    v3 = v.reshape(B, S, nh * V_WIDTH)
    out = pl.pallas_call(
        functools.partial(_attn_kernel, tq=tq, sub=sub),
        grid=(B, nh, S // tq),
        in_specs=[pl.BlockSpec((1, tq, QK_WIDTH), lambda b, h, i: (b, i, h)),
                  pl.BlockSpec((1, S, QK_WIDTH), lambda b, h, i: (b, 0, h)),
                  pl.BlockSpec((1, S, V_WIDTH), lambda b, h, i: (b, 0, h))],
        out_specs=pl.BlockSpec((1, tq, MLA_V), lambda b, h, i: (b, i, h)),
        out_shape=jax.ShapeDtypeStruct((B, S, nh * MLA_V), BF16),
        scratch_shapes=[pltpu.VMEM((tq, LANES), F32),
                        pltpu.VMEM((tq, V_WIDTH), F32)],
        compiler_params=_cparams(("parallel", "parallel", "arbitrary")),
        name="flash_attn",
    )(q3, k3, v3)
    return out.reshape(B * S, nh * MLA_V)


def _proj_ln_kernel(o_ref, x_ref, w_ref, g_ref, b_ref, out_ref):
    h = _dot(o_ref[...], w_ref[...])
    out_ref[...] = _layer_norm(DEEPNORM_ALPHA * x_ref[...] + h, g_ref[...], b_ref[...])


def _proj_ln(o, x2, w_o, g, b):
    T = x2.shape[0]
    tm = min(TOKEN_TILE, T)
    row = lambda i: (i, 0)
    return pl.pallas_call(
        _proj_ln_kernel,
        grid=(T // tm,),
        in_specs=[pl.BlockSpec((tm, D_MODEL), row), pl.BlockSpec((tm, D_MODEL), row),
                  _const_spec((D_MODEL, D_MODEL)), _const_spec((1, D_MODEL)), _const_spec((1, D_MODEL))],
        out_specs=pl.BlockSpec((tm, D_MODEL), row),
        out_shape=jax.ShapeDtypeStruct((T, D_MODEL), F32),
        compiler_params=_cparams(("parallel",)),
        name="proj_ln",
    )(o, x2, w_o.astype(BF16), g[None, :], b[None, :])


def _ffn_kernel(x_ref, p_ref, win_ref, wdown_ref, g_ref, b_ref, wgate_ref, wproj_ref, out_ref):
    x = x_ref[...]
    xb = x.astype(BF16)
    acc = jnp.zeros(x.shape, F32)
    for c in range(D_FF // FFN_CHUNK):
        lo = c * FFN_CHUNK
        gate = _dot(xb, win_ref[:, lo:lo + FFN_CHUNK])
        up = _dot(xb, win_ref[:, D_FF + lo:D_FF + lo + FFN_CHUNK])
        act = (gate * _sigmoid(gate) * up).astype(BF16)
        acc = acc + _dot(act, wdown_ref[lo:lo + FFN_CHUNK, :])
    y = _layer_norm(DEEPNORM_ALPHA * x + acc, g_ref[...], b_ref[...])
    gate = _sigmoid(_dot(y.astype(BF16), wgate_ref[...]))
    emb = _dot(p_ref[...].astype(BF16), wproj_ref[...])
    out_ref[...] = y + gate * emb


def _ffn_ln_ple(x2, p2, w_in, w_down, g, b, w_gate, w_proj):
    T = x2.shape[0]
    tm = min(TOKEN_TILE, T)
    row = lambda i: (i, 0)
    return pl.pallas_call(
        _ffn_kernel,
        grid=(T // tm,),
        in_specs=[pl.BlockSpec((tm, D_MODEL), row), pl.BlockSpec((tm, D_PLE), row),
                  _const_spec((D_MODEL, 2 * D_FF)), _const_spec((D_FF, D_MODEL)),
                  _const_spec((1, D_MODEL)), _const_spec((1, D_MODEL)),
                  _const_spec((D_MODEL, D_MODEL)), _const_spec((D_PLE, D_MODEL))],
        out_specs=pl.BlockSpec((tm, D_MODEL), row),
        out_shape=jax.ShapeDtypeStruct((T, D_MODEL), F32),
        compiler_params=_cparams(("parallel",)),
        name="ffn_ln_ple",
    )(x2, p2, w_in.astype(BF16), w_down.astype(BF16), g[None, :], b[None, :],
      w_gate.astype(BF16), w_proj.astype(BF16))


def _hgrn_pre_kernel(x_ref, w_ref, lbl_ref, q_out, k_out, lf_out, v_out, sg_out, *, layer):
    n = HGRN_HEADS * HGRN_DK
    logits = lbl_ref[...]
    mx = jnp.max(logits, axis=0, keepdims=True)
    e = jnp.exp(logits - mx)
    soft = e / jnp.sum(e, axis=0, keepdims=True)
    lb = jnp.zeros((1, n), F32)
    for j in range(1, layer + 1):
        lb = lb + soft[j:j + 1, :]
    log_lb = jnp.log(lb)
    log_1m = jnp.log1p(-lb)

    proj = _dot(x_ref[...].astype(BF16), w_ref[...])
    qp = proj[:, :n]
    f = proj[:, n:2 * n]
    gp = proj[:, 3 * n:]
    q_out[...] = qp * _sigmoid(qp)
    ef = jnp.exp(-jnp.abs(f))
    r = 1.0 / (1.0 + ef)
    k_out[...] = (1.0 - lb) * jnp.where(f >= 0, ef * r, r)
    b = log_1m + (jnp.minimum(f, 0.0) - jnp.log1p(ef))
    hi = jnp.maximum(log_lb, b)
    lo = jnp.minimum(log_lb, b)
    lf_out[...] = hi + jnp.log1p(jnp.exp(lo - hi))
    v_out[...] = proj[:, 2 * n:3 * n]
    sg_out[...] = gp * _sigmoid(gp)


def _hgrn_pre(x2, w_in, lb_logits, layer):
    T = x2.shape[0]
    tm = min(TOKEN_TILE, T)
    n = HGRN_HEADS * HGRN_DK
    row = lambda i: (i, 0)
    outs = [jax.ShapeDtypeStruct((T, n), F32)] * 5
    return pl.pallas_call(
        functools.partial(_hgrn_pre_kernel, layer=layer),
        grid=(T // tm,),
        in_specs=[pl.BlockSpec((tm, D_MODEL), row), _const_spec((D_MODEL, 4 * n)),
                  _const_spec((DEPTH, n))],
        out_specs=[pl.BlockSpec((tm, n), row)] * 5,
        out_shape=outs,
        compiler_params=_cparams(("parallel",)),
        name="hgrn_pre",
    )(x2, w_in.astype(BF16), lb_logits)


def _split3(g):
    g1 = g.astype(BF16)
    r1 = g - g1.astype(F32)
    g2 = r1.astype(BF16)
    g3 = (r1 - g2.astype(F32)).astype(BF16)
    return g1, g2, g3


def _level_ref(g_sc, h, chunk):
    pieces = []
    for r in range(chunk // SUBLANES):
        r0 = r * SUBLANES

        def bc(row):
            return jnp.broadcast_to(g_sc[row:row + 1, :], (SUBLANES, LANES))

        if h >= SUBLANES // 2:
            mid = (r0 // (2 * h)) * (2 * h) + h - 1
            pieces.append(bc(mid))
        else:
            sub = lax.broadcasted_iota(jnp.int32, (SUBLANES, LANES), 0)
            piece = bc(r0 + h - 1)
            for blk_start in range(2 * h, SUBLANES, 2 * h):
                piece = jnp.where(sub >= blk_start, bc(r0 + blk_start + h - 1), piece)
            pieces.append(piece)
    return jnp.concatenate(pieces, axis=0)


def _hgrn_rec_kernel(q_ref, k_ref, lf_ref, v_ref, sg_ref, gn_ref, o_ref, st_sc, g_sc, *, chunk, n_chunks):
    @pl.when(pl.program_id(2) == 0)
    def _():
        st_sc[...] = jnp.zeros(st_sc.shape, F32)

    ti = lax.broadcasted_iota(jnp.int32, (chunk, chunk), 0)
    si = lax.broadcasted_iota(jnp.int32, (chunk, chunk), 1)
    tri = (si <= ti).astype(BF16)
    txs = jnp.bitwise_xor(ti, si)
    rowi = lax.broadcasted_iota(jnp.int32, (chunk, LANES), 0)
    gn = gn_ref[...]

    def chunk_body(c, carry):
        rows = pl.ds(pl.multiple_of(c * chunk, chunk), chunk)
        g = lf_ref[0, rows, :]
        q = q_ref[0, rows, :]
        k = k_ref[0, rows, :]
        v = v_ref[0, rows, :]
        g1, g2, g3 = _split3(g)
        G = _dot(tri, g1) + _dot(tri, g2) + _dot(tri, g3)
        g_sc[...] = G
        g_last = G[chunk - 1:chunk, :]

        st = st_sc[...]
        inter = _dot_nt((q * jnp.exp(G)).astype(BF16), st.astype(BF16))

        a = jnp.zeros((chunk, chunk), F32)
        h = chunk // 2
        while h >= 1:
            shift = h.bit_length() - 1
            if h == 1:
                d = jnp.where((rowi & 1) == 1, g, 0.0)
                arg = d
            else:
                d = G - _level_ref(g_sc, h, chunk)
                up = ((rowi >> shift) & 1) == 1
                arg = jnp.where(up, d, -d)
            e = jnp.exp(arg)
            al = _dot_nt((q * e).astype(BF16), (k * e).astype(BF16))
            a = jnp.where(((txs >> shift) == 1) & (ti > si), al, a)
            h //= 2
        a = jnp.where(ti == si, jnp.sum(q * k, axis=1, keepdims=True), a)
        o = inter + _dot(a.astype(BF16), v.astype(BF16))

        kdec = (k * jnp.exp(g_last - G)).astype(BF16)
        upd = _dot(v.T.astype(BF16), kdec)
        st_sc[...] = st * jnp.exp(g_last) + upd

        on = _rms_norm(o, gn)
        o_ref[0, rows, :] = (on * sg_ref[0, rows, :]).astype(o_ref.dtype)
        return carry

    lax.fori_loop(0, n_chunks, chunk_body, 0)


def _hgrn_rec(q, k, lf, v, sg, out_norm, B, S):
    nh = HGRN_HEADS
    ts = min(HGRN_SEQ_TILE, S)
    chunk = min(HGRN_CHUNK, ts)
    n = nh * HGRN_DK
    r3 = lambda t: t.reshape(B, S, n)
    blk = pl.BlockSpec((1, ts, LANES), lambda b, h, i: (b, i, h))
    out = pl.pallas_call(
        functools.partial(_hgrn_rec_kernel, chunk=chunk, n_chunks=ts // chunk),
        grid=(B, nh, S // ts),
        in_specs=[blk, blk, blk, blk, blk, pl.BlockSpec((1, LANES), lambda b, h, i: (0, h))],
        out_specs=blk,
        out_shape=jax.ShapeDtypeStruct((B, S, n), BF16),
        scratch_shapes=[pltpu.VMEM((HGRN_DV, HGRN_DK), F32),
                        pltpu.VMEM((chunk, LANES), F32)],
        compiler_params=_cparams(("parallel", "parallel", "arbitrary")),
        name="hgrn_rec",
    )(r3(q), r3(k), r3(lf), r3(v), r3(sg), out_norm[None, :])
    return out.reshape(B * S, n)


def kernel(x, p, positions, mla_w_dqkv, mla_q_norm, mla_kv_norm, mla_w_uq, mla_w_ukv, mla_w_o,
           hgrn_w_in, hgrn_lb_logits, hgrn_out_norm, hgrn_w_o, ffn_w_in, ffn_w_down,
           ln_mix_g, ln_mix_b, ln_ffn_g, ln_ffn_b, ple_w_proj, ple_w_gate):
    B, S, D = x.shape
    T = B * S
    x2 = x.reshape(T, D)
    pos2 = positions.astype(F32).reshape(T, 1)
    depth = p.shape[0]
    for i in range(depth):
        j = i // N_MIXERS
        if i % N_MIXERS == 0:
            q, k, v = _mla_pre(x2, pos2, mla_w_dqkv[j], mla_q_norm[j], mla_kv_norm[j],
                               mla_w_uq[j], mla_w_ukv[j])
            o = _flash_attn(q, k, v, B, S)
            w_o = mla_w_o[j]
        else:
            q, k, lf, v, sg = _hgrn_pre(x2, hgrn_w_in[j], hgrn_lb_logits, i)
            o = _hgrn_rec(q, k, lf, v, sg, hgrn_out_norm[j], B, S)
            w_o = hgrn_w_o[j]
        x2 = _proj_ln(o, x2, w_o, ln_mix_g[i], ln_mix_b[i])
        x2 = _ffn_ln_ple(x2, p[i].reshape(T, D_PLE), ffn_w_in[i], ffn_w_down[i],
                         ln_ffn_g[i], ln_ffn_b[i], ple_w_gate[i], ple_w_proj[i])
    return x2.reshape(B, S, D)
```

```python
import functools

import jax
import jax.numpy as jnp
from jax import lax
from jax.experimental import pallas as pl
from jax.experimental.pallas import tpu as pltpu

F32 = jnp.float32
BF16 = jnp.bfloat16

D_MODEL = 1024
DEPTH = 2
N_MIXERS = 2
MLA_HEADS = 8
MLA_Q_LORA = 256
MLA_KV_LORA = 256
MLA_NOPE = 128
MLA_ROPE = 64
MLA_V = 128
ROPE_THETA = 10000.0
HGRN_HEADS = 8
HGRN_DK = D_MODEL // HGRN_HEADS
HGRN_DV = D_MODEL // HGRN_HEADS
D_FF = 2816
D_PLE = 256
LN_EPS = 1e-5
RMS_EPS = 1e-6
DEEPNORM_ALPHA = (2 * DEPTH) ** 0.25

LANES = 128
SUBLANES = 8
VMEM_LIMIT_BYTES = 56 * 1024 * 1024

QK_WIDTH = 2 * LANES
NEG_BIG = -1e30

TOKEN_TILE = 512
ATTN_Q_TILE = 2048
ATTN_KV_SUB = 256
V_WIDTH = 2 * LANES
LOG2_E = 1.4426950408889634
HGRN_SEQ_TILE = 1024
HGRN_CHUNK = 128
FFN_CHUNK = 256
HGRN_PRE_COLS = 256
HGRN_PRE_TILE = 512


def _cparams(semantics):
    return pltpu.CompilerParams(dimension_semantics=semantics,
                                vmem_limit_bytes=VMEM_LIMIT_BYTES)


def _const_spec(shape):
    nd = len(shape)
    return pl.BlockSpec(shape, lambda *_: (0,) * nd)


def _sigmoid(x):
    return 1.0 / (1.0 + jnp.exp(-x))


def _layer_norm(y, g, b):
    mu = jnp.mean(y, axis=-1, keepdims=True)
    d = y - mu
    var = jnp.mean(d * d, axis=-1, keepdims=True)
    return d * lax.rsqrt(var + LN_EPS) * g + b


def _rms_norm(t, g):
    return t * lax.rsqrt(jnp.mean(t * t, axis=-1, keepdims=True) + RMS_EPS) * g


def _dot(a, b):
    return jnp.dot(a, b, preferred_element_type=F32)


def _dot_nt(a, b):
    return lax.dot_general(a, b, (((1,), (1,)), ((), ())), preferred_element_type=F32)


def _mla_pre_kernel(x_ref, pos_ref, invf_ref, wd_ref, gq_ref, gkv_ref, wq_ref, wkv_ref,
                    q_out, k_out, v_out, *, scale):
    nh = MLA_HEADS
    xb = x_ref[...].astype(BF16)
    down = _dot(xb, wd_ref[...])
    c_q = _rms_norm(down[:, :MLA_Q_LORA], gq_ref[...]).astype(BF16)
    c_kv = _rms_norm(down[:, MLA_Q_LORA:MLA_Q_LORA + MLA_KV_LORA], gkv_ref[...]).astype(BF16)
    ang = pos_ref[...] * invf_ref[...]
    cos = jnp.cos(ang)
    sin = jnp.sin(ang)
    base = MLA_Q_LORA + MLA_KV_LORA
    kr = (down[:, base:base + LANES] * cos + down[:, base + LANES:base + 2 * LANES] * sin).astype(BF16)

    qall = _dot(c_q, wq_ref[...])
    kvall = _dot(c_kv, wkv_ref[...])
    for h in range(nh):
        lo = h * LANES
        qn = qall[:, lo:lo + LANES]
        qr = qall[:, nh * LANES + lo:nh * LANES + lo + LANES]
        qx = qall[:, 2 * nh * LANES + lo:2 * nh * LANES + lo + LANES]
        q_out[:, h * QK_WIDTH:h * QK_WIDTH + LANES] = (qn * scale).astype(BF16)
        q_out[:, h * QK_WIDTH + LANES:(h + 1) * QK_WIDTH] = ((qr * cos + qx * sin) * scale).astype(BF16)
        k_out[:, h * QK_WIDTH:h * QK_WIDTH + LANES] = kvall[:, lo:lo + LANES].astype(BF16)
        k_out[:, h * QK_WIDTH + LANES:(h + 1) * QK_WIDTH] = kr
        v_out[:, h * V_WIDTH:h * V_WIDTH + LANES] = kvall[:, nh * LANES + lo:nh * LANES + lo + LANES].astype(BF16)
        v_out[:, h * V_WIDTH + LANES:(h + 1) * V_WIDTH] = jnp.ones((x_ref.shape[0], LANES), BF16)


def _rope_slot(w):
    half = MLA_ROPE // 2
    t1, t2 = w[..., :half], w[..., half:]
    z = jnp.zeros(w.shape[:-1] + (LANES - MLA_ROPE,), w.dtype)
    return jnp.concatenate([t1, t2, z], -1), jnp.concatenate([-t2, t1, z], -1)


def _mla_pre(x2, pos2, w_dqkv, q_norm, kv_norm, w_uq, w_ukv):
    T = x2.shape[0]
    tm = min(TOKEN_TILE, T)
    nh = MLA_HEADS
    base = MLA_Q_LORA + MLA_KV_LORA
    slot, rot = _rope_slot(w_dqkv[:, base:])
    wd = jnp.concatenate([w_dqkv[:, :base], slot, rot], axis=1).astype(BF16)
    wq3 = w_uq.reshape(MLA_Q_LORA, nh, MLA_NOPE + MLA_ROPE)
    qslot, qrot = _rope_slot(wq3[:, :, MLA_NOPE:])
    wq = jnp.concatenate([wq3[:, :, :MLA_NOPE].reshape(MLA_Q_LORA, nh * LANES),
                          qslot.reshape(MLA_Q_LORA, nh * LANES),
                          qrot.reshape(MLA_Q_LORA, nh * LANES)], axis=1).astype(BF16)
    wkv3 = w_ukv.reshape(MLA_KV_LORA, nh, MLA_NOPE + MLA_V)
    wkv = jnp.concatenate([wkv3[:, :, :MLA_NOPE].reshape(MLA_KV_LORA, nh * LANES),
                           wkv3[:, :, MLA_NOPE:].reshape(MLA_KV_LORA, nh * LANES)], axis=1).astype(BF16)
    inv_freq = ROPE_THETA ** (-jnp.arange(0, MLA_ROPE, 2, dtype=F32) / MLA_ROPE)
    invf = jnp.concatenate([inv_freq, inv_freq, jnp.zeros((LANES - MLA_ROPE,), F32)])[None, :]
    scale = (MLA_NOPE + MLA_ROPE) ** -0.5 * LOG2_E
    row = lambda i: (i, 0)
    return pl.pallas_call(
        functools.partial(_mla_pre_kernel, scale=scale),
        grid=(T // tm,),
        in_specs=[pl.BlockSpec((tm, D_MODEL), row),
                  pl.BlockSpec((tm, 1), row),
                  _const_spec(invf.shape), _const_spec(wd.shape),
                  _const_spec((1, MLA_Q_LORA)), _const_spec((1, MLA_KV_LORA)),
                  _const_spec(wq.shape), _const_spec(wkv.shape)],
        out_specs=[pl.BlockSpec((tm, nh * QK_WIDTH), row),
                   pl.BlockSpec((tm, nh * QK_WIDTH), row),
                   pl.BlockSpec((tm, nh * V_WIDTH), row)],
        out_shape=[jax.ShapeDtypeStruct((T, nh * QK_WIDTH), BF16),
                   jax.ShapeDtypeStruct((T, nh * QK_WIDTH), BF16),
                   jax.ShapeDtypeStruct((T, nh * V_WIDTH), BF16)],
        compiler_params=_cparams(("parallel",)),
        name="mla_pre",
    )(x2, pos2, invf, wd, q_norm[None, :], kv_norm[None, :], wq, wkv)


def _attn_kernel(q_ref, k_ref, v_ref, o_ref, m_sc, acc_sc, *, tq, sub):
    qi = pl.program_id(2)
    m_sc[...] = jnp.full(m_sc.shape, NEG_BIG, F32)
    acc_sc[...] = jnp.zeros(acc_sc.shape, F32)
    nsub = tq // sub

    def sub_step(kv_start, row0, diag_col0):
        s = _dot_nt(q_ref[0, row0:, :], k_ref[0, pl.ds(kv_start, sub), :])
        if diag_col0 is not None:
            r = lax.broadcasted_iota(jnp.int32, s.shape, 0) + row0
            c = lax.broadcasted_iota(jnp.int32, s.shape, 1) + diag_col0
            s = jnp.where(c <= r, s, NEG_BIG)
        m_prev = m_sc[row0:, :]
        m_new = jnp.maximum(m_prev, jnp.max(s, axis=1, keepdims=True))
        alpha = jnp.exp2(m_prev - m_new)
        p = jnp.exp2(s - jnp.tile(m_new, (1, sub // LANES))).astype(BF16)
        v = v_ref[0, pl.ds(kv_start, sub), :]
        acc_sc[row0:, :] = jnp.tile(alpha, (1, 2)) * acc_sc[row0:, :] + _dot(p, v)
        m_sc[row0:, :] = m_new

    def body(j, carry):
        base = j * tq
        for c in range(nsub):
            sub_step(pl.multiple_of(base + c * sub, sub), 0, None)
        return carry

    lax.fori_loop(0, qi, body, 0)
    base = qi * tq
    for c in range(nsub):
        sub_step(pl.multiple_of(base + c * sub, sub), c * sub, c * sub)
    acc = acc_sc[...]
    o_ref[0] = (acc[:, :MLA_V] / acc[:, MLA_V:]).astype(o_ref.dtype)


def _flash_attn(q, k, v, B, S):
    nh = MLA_HEADS
    tq = min(ATTN_Q_TILE, S)
    sub = min(ATTN_KV_SUB, tq)
    q3 = q.reshape(B, S, nh * QK_WIDTH)
    k3 = k.reshape(B, S, nh * QK_WIDTH)
    v3 = v.reshape(B, S, nh * V_WIDTH)
    out = pl.pallas_call(
        functools.partial(_attn_kernel, tq=tq, sub=sub),
        grid=(B, nh, S // tq),
        in_specs=[pl.BlockSpec((1, tq, QK_WIDTH), lambda b, h, i: (b, i, h)),
                  pl.BlockSpec((1, S, QK_WIDTH), lambda b, h, i: (b, 0, h)),
                  pl.BlockSpec((1, S, V_WIDTH), lambda b, h, i: (b, 0, h))],
        out_specs=pl.BlockSpec((1, tq, MLA_V), lambda b, h, i: (b, i, h)),
        out_shape=jax.ShapeDtypeStruct((B, S, nh * MLA_V), BF16),
        scratch_shapes=[pltpu.VMEM((tq, LANES), F32),
                        pltpu.VMEM((tq, V_WIDTH), F32)],
        compiler_params=_cparams(("parallel", "parallel", "arbitrary")),
        name="flash_attn",
    )(q3, k3, v3)
    return out.reshape(B * S, nh * MLA_V)


def _proj_ln_kernel(o_ref, x_ref, w_ref, g_ref, b_ref, out_ref):
    h = _dot(o_ref[...], w_ref[...])
    out_ref[...] = _layer_norm(DEEPNORM_ALPHA * x_ref[...] + h, g_ref[...], b_ref[...])


def _proj_ln(o, x2, w_o, g, b):
    T = x2.shape[0]
    tm = min(TOKEN_TILE, T)
    row = lambda i: (i, 0)
    return pl.pallas_call(
        _proj_ln_kernel,
        grid=(T // tm,),
        in_specs=[pl.BlockSpec((tm, D_MODEL), row), pl.BlockSpec((tm, D_MODEL), row),
                  _const_spec((D_MODEL, D_MODEL)), _const_spec((1, D_MODEL)), _const_spec((1, D_MODEL))],
        out_specs=pl.BlockSpec((tm, D_MODEL), row),
        out_shape=jax.ShapeDtypeStruct((T, D_MODEL), F32),
        compiler_params=_cparams(("parallel",)),
        name="proj_ln",
    )(o, x2, w_o.astype(BF16), g[None, :], b[None, :])


def _ffn_kernel(x_ref, p_ref, win_ref, wdown_ref, g_ref, b_ref, wgate_ref, wproj_ref, out_ref):
    x = x_ref[...]
    xb = x.astype(BF16)
    acc = jnp.zeros(x.shape, F32)
    for c in range(D_FF // FFN_CHUNK):
        lo = c * FFN_CHUNK
        gate = _dot(xb, win_ref[:, lo:lo + FFN_CHUNK])
        up = _dot(xb, win_ref[:, D_FF + lo:D_FF + lo + FFN_CHUNK])
        act = (gate * _sigmoid(gate) * up).astype(BF16)
        acc = acc + _dot(act, wdown_ref[lo:lo + FFN_CHUNK, :])
    y = _layer_norm(DEEPNORM_ALPHA * x + acc, g_ref[...], b_ref[...])
    gate = _sigmoid(_dot(y.astype(BF16), wgate_ref[...]))
    emb = _dot(p_ref[...].astype(BF16), wproj_ref[...])
    out_ref[...] = y + gate * emb


def _ffn_ln_ple(x2, p2, w_in, w_down, g, b, w_gate, w_proj):
    T = x2.shape[0]
    tm = min(TOKEN_TILE, T)
    row = lambda i: (i, 0)
    return pl.pallas_call(
        _ffn_kernel,
        grid=(T // tm,),
        in_specs=[pl.BlockSpec((tm, D_MODEL), row), pl.BlockSpec((tm, D_PLE), row),
                  _const_spec((D_MODEL, 2 * D_FF)), _const_spec((D_FF, D_MODEL)),
                  _const_spec((1, D_MODEL)), _const_spec((1, D_MODEL)),
                  _const_spec((D_MODEL, D_MODEL)), _const_spec((D_PLE, D_MODEL))],
        out_specs=pl.BlockSpec((tm, D_MODEL), row),
        out_shape=jax.ShapeDtypeStruct((T, D_MODEL), F32),
        compiler_params=_cparams(("parallel",)),
        name="ffn_ln_ple",
    )(x2, p2, w_in.astype(BF16), w_down.astype(BF16), g[None, :], b[None, :],
      w_gate.astype(BF16), w_proj.astype(BF16))


def _hgrn_pre_kernel(x_ref, w_ref, lbl_ref, q_out, k_out, lf_out, v_out, sg_out, *, layer):
    n = HGRN_HEADS * HGRN_DK
    logits = lbl_ref[...]
    mx = jnp.max(logits, axis=0, keepdims=True)
    e = jnp.exp(logits - mx)
    soft = e / jnp.sum(e, axis=0, keepdims=True)
    lb = jnp.zeros((1, n), F32)
    for j in range(1, layer + 1):
        lb = lb + soft[j:j + 1, :]
    log_lb = jnp.log(lb)
    log_1m = jnp.log1p(-lb)

    xb = x_ref[...].astype(BF16)
    blocks = [slice(lo_c, lo_c + HGRN_PRE_COLS) for lo_c in range(0, n, HGRN_PRE_COLS)]

    def proj(part, cols):
        return _dot(xb, w_ref[:, part * n + cols.start:part * n + cols.stop])

    for cols in blocks:
        f = proj(1, cols)
        ef = jnp.exp(-jnp.abs(f))
        r = 1.0 / (1.0 + ef)
        k_out[:, cols] = (1.0 - lb[:, cols]) * jnp.where(f >= 0, ef * r, r)
        b = log_1m[:, cols] + (jnp.minimum(f, 0.0) - jnp.log(1.0 + ef))
        a = log_lb[:, cols]
        lf_out[:, cols] = jnp.maximum(a, b) + jnp.log(1.0 + jnp.exp(-jnp.abs(a - b)))
    for cols in blocks:
        qp = proj(0, cols)
        q_out[:, cols] = qp * _sigmoid(qp)
    for cols in blocks:
        gp = proj(3, cols)
        sg_out[:, cols] = gp * _sigmoid(gp)
    for cols in blocks:
        v_out[:, cols] = proj(2, cols).astype(BF16)


def _hgrn_pre(x2, w_in, lb_logits, layer):
    T = x2.shape[0]
    tm = min(HGRN_PRE_TILE, T)
    n = HGRN_HEADS * HGRN_DK
    row = lambda i: (i, 0)
    outs = [jax.ShapeDtypeStruct((T, n), dt) for dt in (F32, F32, F32, BF16, F32)]
    return pl.pallas_call(
        functools.partial(_hgrn_pre_kernel, layer=layer),
        grid=(T // tm,),
        in_specs=[pl.BlockSpec((tm, D_MODEL), row), _const_spec((D_MODEL, 4 * n)),
                  _const_spec((DEPTH, n))],
        out_specs=[pl.BlockSpec((tm, n), row)] * 5,
        out_shape=outs,
        compiler_params=_cparams(("parallel",)),
        name="hgrn_pre",
    )(x2, w_in.astype(BF16), lb_logits)


def _split3(g):
    g1 = g.astype(BF16)
    r1 = g - g1.astype(F32)
    g2 = r1.astype(BF16)
    g3 = (r1 - g2.astype(F32)).astype(BF16)
    return g1, g2, g3


def _level_operand(h, G, g, q, k, g_sc, rowi):
    chunk = G.shape[0]

    def bc(row):
        return jnp.broadcast_to(g_sc[row:row + 1, :], (SUBLANES, LANES))

    if h >= SUBLANES:
        pieces = []
        for r0 in range(0, chunk, SUBLANES):
            ref = bc((r0 // (2 * h)) * (2 * h) + h - 1)
            rows = slice(r0, r0 + SUBLANES)
            if (r0 // h) % 2 == 1:
                pieces.append(q[rows] * jnp.exp2(G[rows] - ref))
            else:
                pieces.append(k[rows] * jnp.exp2(ref - G[rows]))
        return jnp.concatenate(pieces, axis=0)

    up = (rowi & h) != 0
    if h == 1:
        arg = jnp.where(up, g, 0.0)
    else:
        sub = lax.broadcasted_iota(jnp.int32, (SUBLANES, LANES), 0)
        pieces = []
        for r0 in range(0, chunk, SUBLANES):
            ref = bc(r0 + h - 1)
            for start in range(2 * h, SUBLANES, 2 * h):
                ref = jnp.where(sub >= start, bc(r0 + start + h - 1), ref)
            pieces.append(ref)
        d = G - jnp.concatenate(pieces, axis=0)
        arg = jnp.where(up, d, -d)
    return jnp.where(up, q, k) * jnp.exp2(arg)


def _hgrn_rec_kernel(q_ref, k_ref, lf_ref, v_ref, sg_ref, gn_ref, o_ref,
                     st_sc, g_sc, a_sc, qg_sc, kd_sc, *, chunk, n_chunks):
    @pl.when(pl.program_id(2) == 0)
    def _():
        st_sc[...] = jnp.zeros(st_sc.shape, F32)

    ti = lax.broadcasted_iota(jnp.int32, (chunk, chunk), 0)
    si = lax.broadcasted_iota(jnp.int32, (chunk, chunk), 1)
    tri = (si <= ti).astype(BF16)
    lvl = jnp.where(ti >= si, 31 - lax.clz(jnp.bitwise_xor(ti, si)), -2)
    rowi = lax.broadcasted_iota(jnp.int32, (chunk, LANES), 0)
    gn = gn_ref[...]

    def rows_of(c):
        return slice(c * chunk, (c + 1) * chunk)

    def cumulate(c):
        rows = rows_of(c)
        g1, g2, g3 = _split3(lf_ref[0, rows, :] * LOG2_E)
        g_sc[rows, :] = _dot(tri, g1) + _dot(tri, g2) + _dot(tri, g3)

    def decay_matrix(c):
        rows = rows_of(c)
        g = lf_ref[0, rows, :] * LOG2_E
        G = g_sc[rows, :]
        q = q_ref[0, rows, :]
        k = k_ref[0, rows, :]
        g_last = G[chunk - 1:chunk, :]
        qg_sc[rows, :] = (q * jnp.exp2(G)).astype(BF16)
        kd_sc[rows, :] = (k * jnp.exp2(g_last - G)).astype(BF16)
        a = jnp.where(lvl == -1, jnp.sum(q * k, axis=1, keepdims=True), 0.0)
        h = chunk // 2
        while h >= 1:
            z = _level_operand(h, G, g, q, k, g_sc.at[rows, :], rowi).astype(BF16)
            a = jnp.where(lvl == h.bit_length() - 1, _dot_nt(z, z), a)
            h //= 2
        a_sc[rows, :] = a.astype(BF16)

    def advance(c, st):
        rows = rows_of(c)
        v = v_ref[0, rows, :]
        o = _dot_nt(qg_sc[rows, :], st.astype(BF16)) + _dot(a_sc[rows, :], v)
        upd = _dot(v.astype(F32).T.astype(BF16), kd_sc[rows, :])
        st = st * jnp.exp2(g_sc[(c + 1) * chunk - 1:(c + 1) * chunk, :]) + upd
        on = _rms_norm(o, gn)
        o_ref[0, rows, :] = (on * sg_ref[0, rows, :]).astype(o_ref.dtype)
        return st

    st = st_sc[...]
    for step in range(n_chunks + 2):
        if step < n_chunks:
            cumulate(step)
        if 1 <= step <= n_chunks:
            decay_matrix(step - 1)
        if step >= 2:
            st = advance(step - 2, st)
    st_sc[...] = st


def _hgrn_rec(q, k, lf, v, sg, out_norm, B, S):
    nh = HGRN_HEADS
    ts = min(HGRN_SEQ_TILE, S)
    chunk = min(HGRN_CHUNK, ts)
    n = nh * HGRN_DK
    r3 = lambda t: t.reshape(B, S, n)
    blk = pl.BlockSpec((1, ts, LANES), lambda b, h, i: (b, i, h))
    out = pl.pallas_call(
        functools.partial(_hgrn_rec_kernel, chunk=chunk, n_chunks=ts // chunk),
        grid=(B, nh, S // ts),
        in_specs=[blk, blk, blk, blk, blk, pl.BlockSpec((1, LANES), lambda b, h, i: (0, h))],
        out_specs=blk,
        out_shape=jax.ShapeDtypeStruct((B, S, n), BF16),
        scratch_shapes=[pltpu.VMEM((HGRN_DV, HGRN_DK), F32),
                        pltpu.VMEM((ts, LANES), F32),
                        pltpu.VMEM((ts, chunk), BF16),
                        pltpu.VMEM((ts, LANES), BF16),
                        pltpu.VMEM((ts, LANES), BF16)],
        compiler_params=_cparams(("parallel", "parallel", "arbitrary")),
        name="hgrn_rec",
    )(r3(q), r3(k), r3(lf), r3(v), r3(sg), out_norm[None, :])
    return out.reshape(B * S, n)


def kernel(x, p, positions, mla_w_dqkv, mla_q_norm, mla_kv_norm, mla_w_uq, mla_w_ukv, mla_w_o,
           hgrn_w_in, hgrn_lb_logits, hgrn_out_norm, hgrn_w_o, ffn_w_in, ffn_w_down,
           ln_mix_g, ln_mix_b, ln_ffn_g, ln_ffn_b, ple_w_proj, ple_w_gate):
    B, S, D = x.shape
    T = B * S
    x2 = x.reshape(T, D)
    pos2 = positions.astype(F32).reshape(T, 1)
    depth = p.shape[0]
    for i in range(depth):
        j = i // N_MIXERS
        if i % N_MIXERS == 0:
            q, k, v = _mla_pre(x2, pos2, mla_w_dqkv[j], mla_q_norm[j], mla_kv_norm[j],
                               mla_w_uq[j], mla_w_ukv[j])
            o = _flash_attn(q, k, v, B, S)
            w_o = mla_w_o[j]
        else:
            q, k, lf, v, sg = _hgrn_pre(x2, hgrn_w_in[j], hgrn_lb_logits, i)
            o = _hgrn_rec(q, k, lf, v, sg, hgrn_out_norm[j], B, S)
            w_o = hgrn_w_o[j]
        x2 = _proj_ln(o, x2, w_o, ln_mix_g[i], ln_mix_b[i])
        x2 = _ffn_ln_ple(x2, p[i].reshape(T, D_PLE), ffn_w_in[i], ffn_w_down[i],
                         ln_ffn_g[i], ln_ffn_b[i], ple_w_gate[i], ple_w_proj[i])
    return x2.reshape(B, S, D)
```

```python
import functools

import jax
import jax.numpy as jnp
from jax import lax
from jax.experimental import pallas as pl
from jax.experimental.pallas import tpu as pltpu

F32 = jnp.float32
BF16 = jnp.bfloat16

D_MODEL = 1024
DEPTH = 2
N_MIXERS = 2
MLA_HEADS = 8
MLA_Q_LORA = 256
MLA_KV_LORA = 256
MLA_NOPE = 128
MLA_ROPE = 64
MLA_V = 128
ROPE_THETA = 10000.0
HGRN_HEADS = 8
HGRN_DK = D_MODEL // HGRN_HEADS
HGRN_DV = D_MODEL // HGRN_HEADS
D_FF = 2816
D_PLE = 256
LN_EPS = 1e-5
RMS_EPS = 1e-6
DEEPNORM_ALPHA = (2 * DEPTH) ** 0.25

LANES = 128
SUBLANES = 8
VMEM_LIMIT_BYTES = 56 * 1024 * 1024

QK_WIDTH = 2 * LANES
NEG_BIG = -1e30

TOKEN_TILE = 512
ATTN_Q_TILE = 2048
ATTN_KV_SUB = 256
V_WIDTH = 2 * LANES
LOG2_E = 1.4426950408889634
HGRN_SEQ_TILE = 1024
HGRN_CHUNK = 128
FFN_CHUNK = 256
HGRN_PRE_COLS = 256
HGRN_PRE_TILE = 512


def _cparams(semantics):
    return pltpu.CompilerParams(dimension_semantics=semantics,
                                vmem_limit_bytes=VMEM_LIMIT_BYTES)


def _const_spec(shape):
    nd = len(shape)
    return pl.BlockSpec(shape, lambda *_: (0,) * nd, pipeline_mode=pl.Buffered(1))


def _head_major_rows(i):
    return (0, i, 0)


def _sigmoid(x):
    return 1.0 / (1.0 + jnp.exp(-x))


def _layer_norm(y, g, b):
    mu = jnp.mean(y, axis=-1, keepdims=True)
    d = y - mu
    var = jnp.mean(d * d, axis=-1, keepdims=True)
    return d * lax.rsqrt(var + LN_EPS) * g + b


def _rms_norm(t, g):
    return t * lax.rsqrt(jnp.mean(t * t, axis=-1, keepdims=True) + RMS_EPS) * g


def _dot(a, b):
    return jnp.dot(a, b, preferred_element_type=F32)


def _dot_nt(a, b):
    return lax.dot_general(a, b, (((1,), (1,)), ((), ())), preferred_element_type=F32)


def _mla_pre_kernel(x_ref, pos_ref, invf_ref, wd_ref, gq_ref, gkv_ref, wq_ref, wkv_ref,
                    q_out, k_out, v_out, *, scale):
    nh = MLA_HEADS
    xb = x_ref[...].astype(BF16)
    down = _dot(xb, wd_ref[...])
    c_q = _rms_norm(down[:, :MLA_Q_LORA], gq_ref[...]).astype(BF16)
    c_kv = _rms_norm(down[:, MLA_Q_LORA:MLA_Q_LORA + MLA_KV_LORA], gkv_ref[...]).astype(BF16)
    ang = pos_ref[...] * invf_ref[...]
    cos = jnp.cos(ang)
    sin = jnp.sin(ang)
    base = MLA_Q_LORA + MLA_KV_LORA
    kr = (down[:, base:base + LANES] * cos + down[:, base + LANES:base + 2 * LANES] * sin).astype(BF16)

    qall = _dot(c_q, wq_ref[...])
    kvall = _dot(c_kv, wkv_ref[...])
    for h in range(nh):
        lo = h * LANES
        qn = qall[:, lo:lo + LANES]
        qr = qall[:, nh * LANES + lo:nh * LANES + lo + LANES]
        qx = qall[:, 2 * nh * LANES + lo:2 * nh * LANES + lo + LANES]
        q_out[h, :, :LANES] = (qn * scale).astype(BF16)
        q_out[h, :, LANES:] = ((qr * cos + qx * sin) * scale).astype(BF16)
        k_out[h, :, :LANES] = kvall[:, lo:lo + LANES].astype(BF16)
        k_out[h, :, LANES:] = kr
        v_out[h, :, :LANES] = kvall[:, nh * LANES + lo:nh * LANES + lo + LANES].astype(BF16)
        v_out[h, :, LANES:] = jnp.ones((x_ref.shape[0], LANES), BF16)


def _rope_slot(w):
    half = MLA_ROPE // 2
    t1, t2 = w[..., :half], w[..., half:]
    z = jnp.zeros(w.shape[:-1] + (LANES - MLA_ROPE,), w.dtype)
    return jnp.concatenate([t1, t2, z], -1), jnp.concatenate([-t2, t1, z], -1)


def _mla_pre(x2, pos2, w_dqkv, q_norm, kv_norm, w_uq, w_ukv):
    T = x2.shape[0]
    tm = min(TOKEN_TILE, T)
    nh = MLA_HEADS
    base = MLA_Q_LORA + MLA_KV_LORA
    slot, rot = _rope_slot(w_dqkv[:, base:])
    wd = jnp.concatenate([w_dqkv[:, :base], slot, rot], axis=1).astype(BF16)
    wq3 = w_uq.reshape(MLA_Q_LORA, nh, MLA_NOPE + MLA_ROPE)
    qslot, qrot = _rope_slot(wq3[:, :, MLA_NOPE:])
    wq = jnp.concatenate([wq3[:, :, :MLA_NOPE].reshape(MLA_Q_LORA, nh * LANES),
                          qslot.reshape(MLA_Q_LORA, nh * LANES),
                          qrot.reshape(MLA_Q_LORA, nh * LANES)], axis=1).astype(BF16)
    wkv3 = w_ukv.reshape(MLA_KV_LORA, nh, MLA_NOPE + MLA_V)
    wkv = jnp.concatenate([wkv3[:, :, :MLA_NOPE].reshape(MLA_KV_LORA, nh * LANES),
                           wkv3[:, :, MLA_NOPE:].reshape(MLA_KV_LORA, nh * LANES)], axis=1).astype(BF16)
    inv_freq = ROPE_THETA ** (-jnp.arange(0, MLA_ROPE, 2, dtype=F32) / MLA_ROPE)
    invf = jnp.concatenate([inv_freq, inv_freq, jnp.zeros((LANES - MLA_ROPE,), F32)])[None, :]
    scale = (MLA_NOPE + MLA_ROPE) ** -0.5 * LOG2_E
    row = lambda i: (i, 0)
    return pl.pallas_call(
        functools.partial(_mla_pre_kernel, scale=scale),
        grid=(T // tm,),
        in_specs=[pl.BlockSpec((tm, D_MODEL), row),
                  pl.BlockSpec((tm, 1), row),
                  _const_spec(invf.shape), _const_spec(wd.shape),
                  _const_spec((1, MLA_Q_LORA)), _const_spec((1, MLA_KV_LORA)),
                  _const_spec(wq.shape), _const_spec(wkv.shape)],
        out_specs=[pl.BlockSpec((nh, tm, QK_WIDTH), _head_major_rows),
                   pl.BlockSpec((nh, tm, QK_WIDTH), _head_major_rows),
                   pl.BlockSpec((nh, tm, V_WIDTH), _head_major_rows)],
        out_shape=[jax.ShapeDtypeStruct((nh, T, QK_WIDTH), BF16),
                   jax.ShapeDtypeStruct((nh, T, QK_WIDTH), BF16),
                   jax.ShapeDtypeStruct((nh, T, V_WIDTH), BF16)],
        compiler_params=_cparams(("parallel",)),
        name="mla_pre",
    )(x2, pos2, invf, wd, q_norm[None, :], kv_norm[None, :], wq, wkv)


def _attn_kernel(q_ref, k_ref, v_ref, o_ref, m_sc, acc_sc, *, tq, sub):
    qi = pl.program_id(2)
    m_sc[...] = jnp.full(m_sc.shape, NEG_BIG, F32)
    acc_sc[...] = jnp.zeros(acc_sc.shape, F32)
    nsub = tq // sub

    def sub_step(kv_start, row0, diag_col0):
        s = _dot_nt(q_ref[0, 0, row0:, :], k_ref[0, 0, pl.ds(kv_start, sub), :])
        if diag_col0 is not None:
            r = lax.broadcasted_iota(jnp.int32, s.shape, 0) + row0
            c = lax.broadcasted_iota(jnp.int32, s.shape, 1) + diag_col0
            s = jnp.where(c <= r, s, NEG_BIG)
        m_prev = m_sc[row0:, :]
        m_new = jnp.maximum(m_prev, jnp.max(s, axis=1, keepdims=True))
        alpha = jnp.exp2(m_prev - m_new)
        p = jnp.exp2(s - jnp.tile(m_new, (1, sub // LANES))).astype(BF16)
        v = v_ref[0, 0, pl.ds(kv_start, sub), :]
        acc_sc[row0:, :] = jnp.tile(alpha, (1, 2)) * acc_sc[row0:, :] + _dot(p, v)
        m_sc[row0:, :] = m_new

    def body(j, carry):
        base = j * tq
        for c in range(nsub):
            sub_step(pl.multiple_of(base + c * sub, sub), 0, None)
        return carry

    lax.fori_loop(0, qi, body, 0)
    base = qi * tq
    for c in range(nsub):
        sub_step(pl.multiple_of(base + c * sub, sub), c * sub, c * sub)
    acc = acc_sc[...]
    o_ref[0, 0] = (acc[:, :MLA_V] / acc[:, MLA_V:]).astype(o_ref.dtype)


def _flash_attn(q, k, v, B, S):
    nh = MLA_HEADS
    tq = min(ATTN_Q_TILE, S)
    sub = min(ATTN_KV_SUB, tq)
    q4 = q.reshape(nh, B, S, QK_WIDTH)
    k4 = k.reshape(nh, B, S, QK_WIDTH)
    v4 = v.reshape(nh, B, S, V_WIDTH)
    whole_seq = lambda b, h, i: (h, b, 0, 0)
    q_tile = lambda b, h, i: (h, b, i, 0)
    out = pl.pallas_call(
        functools.partial(_attn_kernel, tq=tq, sub=sub),
        grid=(B, nh, S // tq),
        in_specs=[pl.BlockSpec((1, 1, tq, QK_WIDTH), q_tile),
                  pl.BlockSpec((1, 1, S, QK_WIDTH), whole_seq),
                  pl.BlockSpec((1, 1, S, V_WIDTH), whole_seq)],
        out_specs=pl.BlockSpec((1, 1, tq, MLA_V), q_tile),
        out_shape=jax.ShapeDtypeStruct((nh, B, S, MLA_V), BF16),
        scratch_shapes=[pltpu.VMEM((tq, LANES), F32),
                        pltpu.VMEM((tq, V_WIDTH), F32)],
        compiler_params=_cparams(("parallel", "parallel", "arbitrary")),
        name="flash_attn",
    )(q4, k4, v4)
    return out.reshape(nh, B * S, MLA_V)


def _layer_tail_kernel(o_ref, x_ref, p_ref, wo_ref, g1_ref, b1_ref, win_ref, wdown_ref, g_ref, b_ref,
                       wgate_ref, wproj_ref, out_ref):
    o = jnp.concatenate([o_ref[h] for h in range(o_ref.shape[0])], axis=1)
    x = _layer_norm(DEEPNORM_ALPHA * x_ref[...] + _dot(o, wo_ref[...]), g1_ref[...], b1_ref[...])
    xb = x.astype(BF16)
    acc = jnp.zeros(x.shape, F32)
    for c in range(D_FF // FFN_CHUNK):
        lo = c * FFN_CHUNK
        gate = _dot(xb, win_ref[:, lo:lo + FFN_CHUNK])
        up = _dot(xb, win_ref[:, D_FF + lo:D_FF + lo + FFN_CHUNK])
        act = (gate * _sigmoid(gate) * up).astype(BF16)
        acc = acc + _dot(act, wdown_ref[lo:lo + FFN_CHUNK, :])
    y = _layer_norm(DEEPNORM_ALPHA * x + acc, g_ref[...], b_ref[...])
    gate = _sigmoid(_dot(y.astype(BF16), wgate_ref[...]))
    emb = _dot(p_ref[...].astype(BF16), wproj_ref[...])
    out_ref[...] = y + gate * emb


def _layer_tail(o, x2, p2, w_o, g1, b1, w_in, w_down, g2, b2, w_gate, w_proj):
    T = x2.shape[0]
    tm = min(TOKEN_TILE, T)
    row = lambda i: (i, 0)
    vec = _const_spec((1, D_MODEL))
    return pl.pallas_call(
        _layer_tail_kernel,
        grid=(T // tm,),
        in_specs=[pl.BlockSpec((o.shape[0], tm, o.shape[2]), _head_major_rows),
                  pl.BlockSpec((tm, D_MODEL), row), pl.BlockSpec((tm, D_PLE), row),
                  _const_spec((D_MODEL, D_MODEL)), vec, vec,
                  _const_spec((D_MODEL, 2 * D_FF)), _const_spec((D_FF, D_MODEL)), vec, vec,
                  _const_spec((D_MODEL, D_MODEL)), _const_spec((D_PLE, D_MODEL))],
        out_specs=pl.BlockSpec((tm, D_MODEL), row),
        out_shape=jax.ShapeDtypeStruct((T, D_MODEL), F32),
        compiler_params=_cparams(("parallel",)),
        name="layer_tail",
    )(o, x2, p2, w_o.astype(BF16), g1[None, :], b1[None, :], w_in.astype(BF16), w_down.astype(BF16),
      g2[None, :], b2[None, :], w_gate.astype(BF16), w_proj.astype(BF16))


def _hgrn_pre_kernel(x_ref, w_ref, lbl_ref, q_out, k_out, lf_out, v_out, sg_out, *, layer):
    n = HGRN_HEADS * HGRN_DK
    logits = lbl_ref[...]
    mx = jnp.max(logits, axis=0, keepdims=True)
    e = jnp.exp(logits - mx)
    soft = e / jnp.sum(e, axis=0, keepdims=True)
    lb = jnp.zeros((1, n), F32)
    for j in range(1, layer + 1):
        lb = lb + soft[j:j + 1, :]
    log_lb = jnp.log(lb)
    log_1m = jnp.log1p(-lb)

    xb = x_ref[...].astype(BF16)
    blocks = [slice(lo_c, lo_c + HGRN_PRE_COLS) for lo_c in range(0, n, HGRN_PRE_COLS)]

    def proj(part, cols):
        return _dot(xb, w_ref[:, part * n + cols.start:part * n + cols.stop])

    def put(out_ref, cols, val):
        for lo_c in range(cols.start, cols.stop, HGRN_DK):
            out_ref[lo_c // HGRN_DK] = val[:, lo_c - cols.start:lo_c - cols.start + HGRN_DK].astype(out_ref.dtype)

    for cols in blocks:
        f = proj(1, cols)
        ef = jnp.exp(-jnp.abs(f))
        r = 1.0 / (1.0 + ef)
        put(k_out, cols, (1.0 - lb[:, cols]) * jnp.where(f >= 0, ef * r, r))
        b = log_1m[:, cols] + (jnp.minimum(f, 0.0) - jnp.log(1.0 + ef))
        a = log_lb[:, cols]
        put(lf_out, cols, jnp.maximum(a, b) + jnp.log(1.0 + jnp.exp(-jnp.abs(a - b))))
    for cols in blocks:
        qp = proj(0, cols)
        put(q_out, cols, qp * _sigmoid(qp))
    for cols in blocks:
        gp = proj(3, cols)
        put(sg_out, cols, gp * _sigmoid(gp))
    for cols in blocks:
        put(v_out, cols, proj(2, cols))


def _hgrn_pre(x2, w_in, lb_logits, layer):
    T = x2.shape[0]
    tm = min(HGRN_PRE_TILE, T)
    n = HGRN_HEADS * HGRN_DK
    row = lambda i: (i, 0)
    outs = [jax.ShapeDtypeStruct((HGRN_HEADS, T, HGRN_DK), dt) for dt in (F32, F32, F32, BF16, F32)]
    return pl.pallas_call(
        functools.partial(_hgrn_pre_kernel, layer=layer),
        grid=(T // tm,),
        in_specs=[pl.BlockSpec((tm, D_MODEL), row), _const_spec((D_MODEL, 4 * n)),
                  _const_spec((DEPTH, n))],
        out_specs=[pl.BlockSpec((HGRN_HEADS, tm, HGRN_DK), _head_major_rows)] * 5,
        out_shape=outs,
        compiler_params=_cparams(("parallel",)),
        name="hgrn_pre",
    )(x2, w_in.astype(BF16), lb_logits)


def _split3(g):
    g1 = g.astype(BF16)
    r1 = g - g1.astype(F32)
    g2 = r1.astype(BF16)
    g3 = (r1 - g2.astype(F32)).astype(BF16)
    return g1, g2, g3


def _level_operand(h, G, g, q, k, g_sc, rowi):
    chunk = G.shape[0]

    def bc(row):
        return jnp.broadcast_to(g_sc[row:row + 1, :], (SUBLANES, LANES))

    if h >= SUBLANES:
        pieces = []
        for r0 in range(0, chunk, SUBLANES):
            ref = bc((r0 // (2 * h)) * (2 * h) + h - 1)
            rows = slice(r0, r0 + SUBLANES)
            if (r0 // h) % 2 == 1:
                pieces.append(q[rows] * jnp.exp2(G[rows] - ref))
            else:
                pieces.append(k[rows] * jnp.exp2(ref - G[rows]))
        return jnp.concatenate(pieces, axis=0)

    up = (rowi & h) != 0
    if h == 1:
        arg = jnp.where(up, g, 0.0)
    else:
        sub = lax.broadcasted_iota(jnp.int32, (SUBLANES, LANES), 0)
        pieces = []
        for r0 in range(0, chunk, SUBLANES):
            ref = bc(r0 + h - 1)
            for start in range(2 * h, SUBLANES, 2 * h):
                ref = jnp.where(sub >= start, bc(r0 + start + h - 1), ref)
            pieces.append(ref)
        d = G - jnp.concatenate(pieces, axis=0)
        arg = jnp.where(up, d, -d)
    return jnp.where(up, q, k) * jnp.exp2(arg)


def _hgrn_rec_kernel(q_ref, k_ref, lf_ref, v_ref, sg_ref, gn_ref, o_ref,
                     st_sc, g_sc, a_sc, qg_sc, kd_sc, *, chunk, n_chunks):
    @pl.when(pl.program_id(2) == 0)
    def _():
        st_sc[...] = jnp.zeros(st_sc.shape, F32)

    ti = lax.broadcasted_iota(jnp.int32, (chunk, chunk), 0)
    si = lax.broadcasted_iota(jnp.int32, (chunk, chunk), 1)
    tri = (si <= ti).astype(BF16)
    lvl = jnp.where(ti >= si, 31 - lax.clz(jnp.bitwise_xor(ti, si)), -2)
    rowi = lax.broadcasted_iota(jnp.int32, (chunk, LANES), 0)
    gn = gn_ref[...]

    def rows_of(c):
        return slice(c * chunk, (c + 1) * chunk)

    def cumulate(c):
        rows = rows_of(c)
        g1, g2, g3 = _split3(lf_ref[0, rows, :] * LOG2_E)
        g_sc[rows, :] = _dot(tri, g1) + _dot(tri, g2) + _dot(tri, g3)

    def decay_matrix(c):
        rows = rows_of(c)
        g = lf_ref[0, rows, :] * LOG2_E
        G = g_sc[rows, :]
        q = q_ref[0, rows, :]
        k = k_ref[0, rows, :]
        g_last = G[chunk - 1:chunk, :]
        qg_sc[rows, :] = (q * jnp.exp2(G)).astype(BF16)
        kd_sc[rows, :] = (k * jnp.exp2(g_last - G)).astype(BF16)
        a = jnp.where(lvl == -1, jnp.sum(q * k, axis=1, keepdims=True), 0.0)
        h = chunk // 2
        while h >= 1:
            z = _level_operand(h, G, g, q, k, g_sc.at[rows, :], rowi).astype(BF16)
            a = jnp.where(lvl == h.bit_length() - 1, _dot_nt(z, z), a)
            h //= 2
        a_sc[rows, :] = a.astype(BF16)

    def advance(c, st):
        rows = rows_of(c)
        v = v_ref[0, rows, :]
        o = _dot_nt(qg_sc[rows, :], st.astype(BF16)) + _dot(a_sc[rows, :], v)
        upd = _dot(v.astype(F32).T.astype(BF16), kd_sc[rows, :])
        st = st * jnp.exp2(g_sc[(c + 1) * chunk - 1:(c + 1) * chunk, :]) + upd
        on = _rms_norm(o, gn)
        o_ref[0, rows, :] = (on * sg_ref[0, rows, :]).astype(o_ref.dtype)
        return st

    st = st_sc[...]
    for step in range(n_chunks + 2):
        if step < n_chunks:
            cumulate(step)
        if 1 <= step <= n_chunks:
            decay_matrix(step - 1)
        if step >= 2:
            st = advance(step - 2, st)
    st_sc[...] = st


def _hgrn_rec(q, k, lf, v, sg, out_norm, B, S):
    nh = HGRN_HEADS
    ts = min(HGRN_SEQ_TILE, S)
    chunk = min(HGRN_CHUNK, ts)
    r4 = lambda t: t.reshape(nh, B, S, HGRN_DK)
    blk = pl.BlockSpec((None, 1, ts, LANES), lambda b, h, i: (h, b, i, 0))
    out = pl.pallas_call(
        functools.partial(_hgrn_rec_kernel, chunk=chunk, n_chunks=ts // chunk),
        grid=(B, nh, S // ts),
        in_specs=[blk, blk, blk, blk, blk, pl.BlockSpec((1, LANES), lambda b, h, i: (0, h))],
        out_specs=blk,
        out_shape=jax.ShapeDtypeStruct((nh, B, S, HGRN_DV), BF16),
        scratch_shapes=[pltpu.VMEM((HGRN_DV, HGRN_DK), F32),
                        pltpu.VMEM((ts, LANES), F32),
                        pltpu.VMEM((ts, chunk), BF16),
                        pltpu.VMEM((ts, LANES), BF16),
                        pltpu.VMEM((ts, LANES), BF16)],
        compiler_params=_cparams(("parallel", "parallel", "arbitrary")),
        name="hgrn_rec",
    )(r4(q), r4(k), r4(lf), r4(v), r4(sg), out_norm[None, :])
    return out.reshape(nh, B * S, HGRN_DV)


def kernel(x, p, positions, mla_w_dqkv, mla_q_norm, mla_kv_norm, mla_w_uq, mla_w_ukv, mla_w_o,
           hgrn_w_in, hgrn_lb_logits, hgrn_out_norm, hgrn_w_o, ffn_w_in, ffn_w_down,
           ln_mix_g, ln_mix_b, ln_ffn_g, ln_ffn_b, ple_w_proj, ple_w_gate):
    B, S, D = x.shape
    T = B * S
    x2 = x.reshape(T, D)
    pos2 = positions.astype(F32).reshape(T, 1)
    depth = p.shape[0]
    for i in range(depth):
        j = i // N_MIXERS
        if i % N_MIXERS == 0:
            q, k, v = _mla_pre(x2, pos2, mla_w_dqkv[j], mla_q_norm[j], mla_kv_norm[j],
                               mla_w_uq[j], mla_w_ukv[j])
            o = _flash_attn(q, k, v, B, S)
            w_o = mla_w_o[j]
        else:
            q, k, lf, v, sg = _hgrn_pre(x2, hgrn_w_in[j], hgrn_lb_logits, i)
            o = _hgrn_rec(q, k, lf, v, sg, hgrn_out_norm[j], B, S)
            w_o = hgrn_w_o[j]
        x2 = _layer_tail(o, x2, p[i].reshape(T, D_PLE), w_o, ln_mix_g[i], ln_mix_b[i],
                         ffn_w_in[i], ffn_w_down[i], ln_ffn_g[i], ln_ffn_b[i],
                         ple_w_gate[i], ple_w_proj[i])
    return x2.reshape(B, S, D)
```

```python
import functools

import jax
import jax.numpy as jnp
from jax import lax
from jax.experimental import pallas as pl
from jax.experimental.pallas import tpu as pltpu

F32 = jnp.float32
BF16 = jnp.bfloat16

D_MODEL = 1024
DEPTH = 2
N_MIXERS = 2
MLA_HEADS = 8
MLA_Q_LORA = 256
MLA_KV_LORA = 256
MLA_NOPE = 128
MLA_ROPE = 64
MLA_V = 128
ROPE_THETA = 10000.0
HGRN_HEADS = 8
HGRN_DK = D_MODEL // HGRN_HEADS
HGRN_DV = D_MODEL // HGRN_HEADS
D_FF = 2816
D_PLE = 256
LN_EPS = 1e-5
RMS_EPS = 1e-6
DEEPNORM_ALPHA = (2 * DEPTH) ** 0.25

LANES = 128
SUBLANES = 8
VMEM_LIMIT_BYTES = 56 * 1024 * 1024

QK_WIDTH = 2 * LANES
NEG_BIG = -1e30

TOKEN_TILE = 512
TAIL_TILE = 512
ATTN_Q_TILE = 2048
ATTN_KV_SUB = 256
V_WIDTH = 2 * LANES
LOG2_E = 1.4426950408889634
HGRN_SEQ_TILE = 1024
HGRN_CHUNK = 128
FFN_CHUNK = 256
HGRN_PRE_COLS = 256
HGRN_PRE_TILE = 512


def _cparams(semantics):
    return pltpu.CompilerParams(dimension_semantics=semantics,
                                vmem_limit_bytes=VMEM_LIMIT_BYTES)


def _const_spec(shape):
    nd = len(shape)
    return pl.BlockSpec(shape, lambda *_: (0,) * nd, pipeline_mode=pl.Buffered(1))


def _head_major_rows(i):
    return (0, i, 0)


def _sigmoid(x):
    return 1.0 / (1.0 + jnp.exp(-x))


def _layer_norm(y, g, b):
    mu = jnp.mean(y, axis=-1, keepdims=True)
    d = y - mu
    var = jnp.mean(d * d, axis=-1, keepdims=True)
    return d * lax.rsqrt(var + LN_EPS) * g + b


def _rms_norm(t, g):
    return t * lax.rsqrt(jnp.mean(t * t, axis=-1, keepdims=True) + RMS_EPS) * g


def _dot(a, b):
    return jnp.dot(a, b, preferred_element_type=F32)


def _dot_nt(a, b):
    return lax.dot_general(a, b, (((1,), (1,)), ((), ())), preferred_element_type=F32)


def _mla_pre_kernel(x_ref, pos_ref, invf_ref, wd_ref, gq_ref, gkv_ref, wq_ref, wkv_ref,
                    q_out, k_out, v_out, *, scale):
    nh = MLA_HEADS
    xb = x_ref[...].astype(BF16)
    down = _dot(xb, wd_ref[...])
    c_q = _rms_norm(down[:, :MLA_Q_LORA], gq_ref[...]).astype(BF16)
    c_kv = _rms_norm(down[:, MLA_Q_LORA:MLA_Q_LORA + MLA_KV_LORA], gkv_ref[...]).astype(BF16)
    ang = pos_ref[...] * invf_ref[...]
    cos = jnp.cos(ang)
    sin = jnp.sin(ang)
    base = MLA_Q_LORA + MLA_KV_LORA
    kr = (down[:, base:base + LANES] * cos + down[:, base + LANES:base + 2 * LANES] * sin).astype(BF16)

    qall = _dot(c_q, wq_ref[...])
    kvall = _dot(c_kv, wkv_ref[...])
    for h in range(nh):
        lo = h * LANES
        qn = qall[:, lo:lo + LANES]
        qr = qall[:, nh * LANES + lo:nh * LANES + lo + LANES]
        qx = qall[:, 2 * nh * LANES + lo:2 * nh * LANES + lo + LANES]
        q_out[h, :, :LANES] = (qn * scale).astype(BF16)
        q_out[h, :, LANES:] = ((qr * cos + qx * sin) * scale).astype(BF16)
        k_out[h, :, :LANES] = kvall[:, lo:lo + LANES].astype(BF16)
        k_out[h, :, LANES:] = kr
        v_out[h, :, :LANES] = kvall[:, nh * LANES + lo:nh * LANES + lo + LANES].astype(BF16)
        v_out[h, :, LANES:] = jnp.ones((x_ref.shape[0], LANES), BF16)


def _rope_slot(w):
    half = MLA_ROPE // 2
    t1, t2 = w[..., :half], w[..., half:]
    z = jnp.zeros(w.shape[:-1] + (LANES - MLA_ROPE,), w.dtype)
    return jnp.concatenate([t1, t2, z], -1), jnp.concatenate([-t2, t1, z], -1)


def _mla_pre(x2, pos2, w_dqkv, q_norm, kv_norm, w_uq, w_ukv):
    T = x2.shape[0]
    tm = min(TOKEN_TILE, T)
    nh = MLA_HEADS
    base = MLA_Q_LORA + MLA_KV_LORA
    slot, rot = _rope_slot(w_dqkv[:, base:])
    wd = jnp.concatenate([w_dqkv[:, :base], slot, rot], axis=1).astype(BF16)
    wq3 = w_uq.reshape(MLA_Q_LORA, nh, MLA_NOPE + MLA_ROPE)
    qslot, qrot = _rope_slot(wq3[:, :, MLA_NOPE:])
    wq = jnp.concatenate([wq3[:, :, :MLA_NOPE].reshape(MLA_Q_LORA, nh * LANES),
                          qslot.reshape(MLA_Q_LORA, nh * LANES),
                          qrot.reshape(MLA_Q_LORA, nh * LANES)], axis=1).astype(BF16)
    wkv3 = w_ukv.reshape(MLA_KV_LORA, nh, MLA_NOPE + MLA_V)
    wkv = jnp.concatenate([wkv3[:, :, :MLA_NOPE].reshape(MLA_KV_LORA, nh * LANES),
                           wkv3[:, :, MLA_NOPE:].reshape(MLA_KV_LORA, nh * LANES)], axis=1).astype(BF16)
    inv_freq = ROPE_THETA ** (-jnp.arange(0, MLA_ROPE, 2, dtype=F32) / MLA_ROPE)
    invf = jnp.concatenate([inv_freq, inv_freq, jnp.zeros((LANES - MLA_ROPE,), F32)])[None, :]
    scale = (MLA_NOPE + MLA_ROPE) ** -0.5 * LOG2_E
    row = lambda i: (i, 0)
    return pl.pallas_call(
        functools.partial(_mla_pre_kernel, scale=scale),
        grid=(T // tm,),
        in_specs=[pl.BlockSpec((tm, D_MODEL), row),
                  pl.BlockSpec((tm, 1), row),
                  _const_spec(invf.shape), _const_spec(wd.shape),
                  _const_spec((1, MLA_Q_LORA)), _const_spec((1, MLA_KV_LORA)),
                  _const_spec(wq.shape), _const_spec(wkv.shape)],
        out_specs=[pl.BlockSpec((nh, tm, QK_WIDTH), _head_major_rows),
                   pl.BlockSpec((nh, tm, QK_WIDTH), _head_major_rows),
                   pl.BlockSpec((nh, tm, V_WIDTH), _head_major_rows)],
        out_shape=[jax.ShapeDtypeStruct((nh, T, QK_WIDTH), BF16),
                   jax.ShapeDtypeStruct((nh, T, QK_WIDTH), BF16),
                   jax.ShapeDtypeStruct((nh, T, V_WIDTH), BF16)],
        compiler_params=_cparams(("parallel",)),
        name="mla_pre",
    )(x2, pos2, invf, wd, q_norm[None, :], kv_norm[None, :], wq, wkv)


def _attn_kernel(q_ref, k_ref, v_ref, o_ref, m_sc, acc_sc, *, tq, sub):
    qi = pl.program_id(2)
    m_sc[...] = jnp.full(m_sc.shape, NEG_BIG, F32)
    acc_sc[...] = jnp.zeros(acc_sc.shape, F32)
    nsub = tq // sub

    def sub_step(kv_start, row0, diag_col0):
        s = _dot_nt(q_ref[0, 0, row0:, :], k_ref[0, 0, pl.ds(kv_start, sub), :])
        if diag_col0 is not None:
            r = lax.broadcasted_iota(jnp.int32, s.shape, 0) + row0
            c = lax.broadcasted_iota(jnp.int32, s.shape, 1) + diag_col0
            s = jnp.where(c <= r, s, NEG_BIG)
        m_prev = m_sc[row0:, :]
        m_new = jnp.maximum(m_prev, jnp.max(s, axis=1, keepdims=True))
        alpha = jnp.exp2(m_prev - m_new)
        p = jnp.exp2(s - jnp.tile(m_new, (1, sub // LANES))).astype(BF16)
        v = v_ref[0, 0, pl.ds(kv_start, sub), :]
        acc_sc[row0:, :] = jnp.tile(alpha, (1, 2)) * acc_sc[row0:, :] + _dot(p, v)
        m_sc[row0:, :] = m_new

    def body(j, carry):
        base = j * tq
        for c in range(nsub):
            sub_step(pl.multiple_of(base + c * sub, sub), 0, None)
        return carry

    lax.fori_loop(0, qi, body, 0)
    base = qi * tq
    for c in range(nsub):
        sub_step(pl.multiple_of(base + c * sub, sub), c * sub, c * sub)
    acc = acc_sc[...]
    o_ref[0, 0] = (acc[:, :MLA_V] / acc[:, MLA_V:]).astype(o_ref.dtype)


def _flash_attn(q, k, v, B, S):
    nh = MLA_HEADS
    tq = min(ATTN_Q_TILE, S)
    sub = min(ATTN_KV_SUB, tq)
    q4 = q.reshape(nh, B, S, QK_WIDTH)
    k4 = k.reshape(nh, B, S, QK_WIDTH)
    v4 = v.reshape(nh, B, S, V_WIDTH)
    whole_seq = lambda b, h, i: (h, b, 0, 0)
    q_tile = lambda b, h, i: (h, b, i, 0)
    out = pl.pallas_call(
        functools.partial(_attn_kernel, tq=tq, sub=sub),
        grid=(B, nh, S // tq),
        in_specs=[pl.BlockSpec((1, 1, tq, QK_WIDTH), q_tile),
                  pl.BlockSpec((1, 1, S, QK_WIDTH), whole_seq),
                  pl.BlockSpec((1, 1, S, V_WIDTH), whole_seq)],
        out_specs=pl.BlockSpec((1, 1, tq, MLA_V), q_tile),
        out_shape=jax.ShapeDtypeStruct((nh, B, S, MLA_V), BF16),
        scratch_shapes=[pltpu.VMEM((tq, LANES), F32),
                        pltpu.VMEM((tq, V_WIDTH), F32)],
        compiler_params=_cparams(("parallel", "parallel", "arbitrary")),
        name="flash_attn",
    )(q4, k4, v4)
    return out.reshape(nh, B * S, MLA_V)


def _layer_tail_kernel(o_ref, x_ref, p_ref, wo_ref, g1_ref, b1_ref, win_ref, wdown_ref, g_ref, b_ref,
                       wgate_ref, wproj_ref, out_ref):
    def mix(rows):
        o = jnp.concatenate([o_ref[h, rows, :] for h in range(o_ref.shape[0])], axis=1)
        return _layer_norm(DEEPNORM_ALPHA * x_ref[rows, :] + _dot(o, wo_ref[...]), g1_ref[...], b1_ref[...])

    n_chunks = D_FF // FFN_CHUNK

    def ffn(xb, acc, first, last):
        for c in range(first, last):
            lo = c * FFN_CHUNK
            gate = _dot(xb, win_ref[:, lo:lo + FFN_CHUNK])
            up = _dot(xb, win_ref[:, D_FF + lo:D_FF + lo + FFN_CHUNK])
            act = (gate * _sigmoid(gate) * up).astype(BF16)
            acc = acc + _dot(act, wdown_ref[lo:lo + FFN_CHUNK, :])
        return acc

    def finish(rows, x, acc):
        y = _layer_norm(DEEPNORM_ALPHA * x + acc, g_ref[...], b_ref[...])
        gate = _sigmoid(_dot(y.astype(BF16), wgate_ref[...]))
        emb = _dot(p_ref[rows, :].astype(BF16), wproj_ref[...])
        out_ref[rows, :] = y + gate * emb

    rows = slice(0, x_ref.shape[0])
    x = mix(rows)
    finish(rows, x, ffn(x.astype(BF16), jnp.zeros(x.shape, F32), 0, n_chunks))


def _layer_spec(layer, shape):
    nd = len(shape)
    return pl.BlockSpec((None,) + tuple(shape), lambda *_: (layer,) + (0,) * nd,
                        pipeline_mode=pl.Buffered(1))


def _layer_tail(layer, o, x2, p3, w_o, g1, b1, w_in, w_down, g2, b2, w_gate, w_proj):
    T = x2.shape[0]
    tm = min(TAIL_TILE, T)
    row = lambda i: (i, 0)
    vec = _layer_spec(layer, (1, D_MODEL))
    return pl.pallas_call(
        _layer_tail_kernel,
        grid=(T // tm,),
        in_specs=[pl.BlockSpec((o.shape[0], tm, o.shape[2]), _head_major_rows),
                  pl.BlockSpec((tm, D_MODEL), row),
                  pl.BlockSpec((None, tm, D_PLE), lambda i: (layer, i, 0)),
                  _const_spec((D_MODEL, D_MODEL)), vec, vec,
                  _layer_spec(layer, (D_MODEL, 2 * D_FF)), _layer_spec(layer, (D_FF, D_MODEL)), vec, vec,
                  _layer_spec(layer, (D_MODEL, D_MODEL)), _layer_spec(layer, (D_PLE, D_MODEL))],
        out_specs=pl.BlockSpec((tm, D_MODEL), row),
        out_shape=jax.ShapeDtypeStruct((T, D_MODEL), F32),
        compiler_params=_cparams(("parallel",)),
        name="layer_tail",
    )(o, x2, p3, w_o.astype(BF16), g1, b1, w_in, w_down, g2, b2, w_gate, w_proj)


def _hgrn_pre_kernel(x_ref, w_ref, lbl_ref, q_out, k_out, lf_out, v_out, sg_out, *, layer):
    n = HGRN_HEADS * HGRN_DK
    logits = lbl_ref[...]
    mx = jnp.max(logits, axis=0, keepdims=True)
    e = jnp.exp(logits - mx)
    soft = e / jnp.sum(e, axis=0, keepdims=True)
    lb = jnp.zeros((1, n), F32)
    for j in range(1, layer + 1):
        lb = lb + soft[j:j + 1, :]
    log_lb = jnp.log(lb)
    log_1m = jnp.log1p(-lb)

    xb = x_ref[...].astype(BF16)
    blocks = [slice(lo_c, lo_c + HGRN_PRE_COLS) for lo_c in range(0, n, HGRN_PRE_COLS)]

    def proj(part, cols):
        return _dot(xb, w_ref[:, part * n + cols.start:part * n + cols.stop])

    def put(out_ref, cols, val):
        for lo_c in range(cols.start, cols.stop, HGRN_DK):
            out_ref[lo_c // HGRN_DK] = val[:, lo_c - cols.start:lo_c - cols.start + HGRN_DK].astype(out_ref.dtype)

    for cols in blocks:
        f = proj(1, cols)
        ef = jnp.exp(-jnp.abs(f))
        r = 1.0 / (1.0 + ef)
        put(k_out, cols, (1.0 - lb[:, cols]) * jnp.where(f >= 0, ef * r, r))
        b = log_1m[:, cols] + (jnp.minimum(f, 0.0) - jnp.log(1.0 + ef))
        a = log_lb[:, cols]
        put(lf_out, cols, jnp.maximum(a, b) + jnp.log(1.0 + jnp.exp(-jnp.abs(a - b))))
    for cols in blocks:
        qp = proj(0, cols)
        put(q_out, cols, qp * _sigmoid(qp))
    for cols in blocks:
        gp = proj(3, cols)
        put(sg_out, cols, gp * _sigmoid(gp))
    for cols in blocks:
        put(v_out, cols, proj(2, cols))


def _hgrn_pre(x2, w_in, lb_logits, layer):
    T = x2.shape[0]
    tm = min(HGRN_PRE_TILE, T)
    n = HGRN_HEADS * HGRN_DK
    row = lambda i: (i, 0)
    outs = [jax.ShapeDtypeStruct((HGRN_HEADS, T, HGRN_DK), dt) for dt in (F32, F32, F32, BF16, F32)]
    return pl.pallas_call(
        functools.partial(_hgrn_pre_kernel, layer=layer),
        grid=(T // tm,),
        in_specs=[pl.BlockSpec((tm, D_MODEL), row), _const_spec((D_MODEL, 4 * n)),
                  _const_spec((DEPTH, n))],
        out_specs=[pl.BlockSpec((HGRN_HEADS, tm, HGRN_DK), _head_major_rows)] * 5,
        out_shape=outs,
        compiler_params=_cparams(("parallel",)),
        name="hgrn_pre",
    )(x2, w_in.astype(BF16), lb_logits)


def _split3(g):
    g1 = g.astype(BF16)
    r1 = g - g1.astype(F32)
    g2 = r1.astype(BF16)
    g3 = (r1 - g2.astype(F32)).astype(BF16)
    return g1, g2, g3


def _level_operand(h, G, g, q, k, g_sc, rowi):
    chunk = G.shape[0]

    def bc(row):
        return jnp.broadcast_to(g_sc[row:row + 1, :], (SUBLANES, LANES))

    if h >= SUBLANES:
        pieces = []
        for r0 in range(0, chunk, SUBLANES):
            ref = bc((r0 // (2 * h)) * (2 * h) + h - 1)
            rows = slice(r0, r0 + SUBLANES)
            if (r0 // h) % 2 == 1:
                pieces.append(q[rows] * jnp.exp2(G[rows] - ref))
            else:
                pieces.append(k[rows] * jnp.exp2(ref - G[rows]))
        return jnp.concatenate(pieces, axis=0)

    up = (rowi & h) != 0
    if h == 1:
        arg = jnp.where(up, g, 0.0)
    else:
        sub = lax.broadcasted_iota(jnp.int32, (SUBLANES, LANES), 0)
        pieces = []
        for r0 in range(0, chunk, SUBLANES):
            ref = bc(r0 + h - 1)
            for start in range(2 * h, SUBLANES, 2 * h):
                ref = jnp.where(sub >= start, bc(r0 + start + h - 1), ref)
            pieces.append(ref)
        d = G - jnp.concatenate(pieces, axis=0)
        arg = jnp.where(up, d, -d)
    return jnp.where(up, q, k) * jnp.exp2(arg)


def _hgrn_rec_kernel(q_ref, k_ref, lf_ref, v_ref, sg_ref, gn_ref, o_ref,
                     st_sc, g_sc, a_sc, qg_sc, kd_sc, *, chunk, n_chunks):
    @pl.when(pl.program_id(2) == 0)
    def _():
        st_sc[...] = jnp.zeros(st_sc.shape, F32)

    ti = lax.broadcasted_iota(jnp.int32, (chunk, chunk), 0)
    si = lax.broadcasted_iota(jnp.int32, (chunk, chunk), 1)
    tri = (si <= ti).astype(BF16)
    lvl = jnp.where(ti >= si, 31 - lax.clz(jnp.bitwise_xor(ti, si)), -2)
    rowi = lax.broadcasted_iota(jnp.int32, (chunk, LANES), 0)
    gn = gn_ref[...]

    def rows_of(c):
        return slice(c * chunk, (c + 1) * chunk)

    def cumulate(c):
        rows = rows_of(c)
        g1, g2, g3 = _split3(lf_ref[0, rows, :] * LOG2_E)
        g_sc[rows, :] = _dot(tri, g1) + _dot(tri, g2) + _dot(tri, g3)

    def decay_matrix(c):
        rows = rows_of(c)
        g = lf_ref[0, rows, :] * LOG2_E
        G = g_sc[rows, :]
        q = q_ref[0, rows, :]
        k = k_ref[0, rows, :]
        g_last = G[chunk - 1:chunk, :]
        qg_sc[rows, :] = (q * jnp.exp2(G)).astype(BF16)
        kd_sc[rows, :] = (k * jnp.exp2(g_last - G)).astype(BF16)
        a = jnp.where(lvl == -1, jnp.sum(q * k, axis=1, keepdims=True), 0.0)
        h = chunk // 2
        while h >= 1:
            z = _level_operand(h, G, g, q, k, g_sc.at[rows, :], rowi).astype(BF16)
            a = jnp.where(lvl == h.bit_length() - 1, _dot_nt(z, z), a)
            h //= 2
        a_sc[rows, :] = a.astype(BF16)

    def advance(c, st):
        rows = rows_of(c)
        v = v_ref[0, rows, :]
        o = _dot_nt(qg_sc[rows, :], st.astype(BF16)) + _dot(a_sc[rows, :], v)
        upd = _dot(v.astype(F32).T.astype(BF16), kd_sc[rows, :])
        st = st * jnp.exp2(g_sc[(c + 1) * chunk - 1:(c + 1) * chunk, :]) + upd
        on = _rms_norm(o, gn)
        o_ref[0, rows, :] = (on * sg_ref[0, rows, :]).astype(o_ref.dtype)
        return st

    st = st_sc[...]
    for step in range(n_chunks + 2):
        if step < n_chunks:
            cumulate(step)
        if 1 <= step <= n_chunks:
            decay_matrix(step - 1)
        if step >= 2:
            st = advance(step - 2, st)
    st_sc[...] = st


def _hgrn_rec(q, k, lf, v, sg, out_norm, B, S):
    nh = HGRN_HEADS
    ts = min(HGRN_SEQ_TILE, S)
    chunk = min(HGRN_CHUNK, ts)
    r4 = lambda t: t.reshape(nh, B, S, HGRN_DK)
    blk = pl.BlockSpec((None, 1, ts, LANES), lambda b, h, i: (h, b, i, 0))
    out = pl.pallas_call(
        functools.partial(_hgrn_rec_kernel, chunk=chunk, n_chunks=ts // chunk),
        grid=(B, nh, S // ts),
        in_specs=[blk, blk, blk, blk, blk, pl.BlockSpec((1, LANES), lambda b, h, i: (0, h))],
        out_specs=blk,
        out_shape=jax.ShapeDtypeStruct((nh, B, S, HGRN_DV), BF16),
        scratch_shapes=[pltpu.VMEM((HGRN_DV, HGRN_DK), F32),
                        pltpu.VMEM((ts, LANES), F32),
                        pltpu.VMEM((ts, chunk), BF16),
                        pltpu.VMEM((ts, LANES), BF16),
                        pltpu.VMEM((ts, LANES), BF16)],
        compiler_params=_cparams(("parallel", "parallel", "arbitrary")),
        name="hgrn_rec",
    )(r4(q), r4(k), r4(lf), r4(v), r4(sg), out_norm[None, :])
    return out.reshape(nh, B * S, HGRN_DV)


def kernel(x, p, positions, mla_w_dqkv, mla_q_norm, mla_kv_norm, mla_w_uq, mla_w_ukv, mla_w_o,
           hgrn_w_in, hgrn_lb_logits, hgrn_out_norm, hgrn_w_o, ffn_w_in, ffn_w_down,
           ln_mix_g, ln_mix_b, ln_ffn_g, ln_ffn_b, ple_w_proj, ple_w_gate):
    B, S, D = x.shape
    T = B * S
    x2 = x.reshape(T, D)
    pos2 = positions.astype(F32).reshape(T, 1)
    depth = p.shape[0]
    p3 = p.reshape(depth, T, D_PLE)
    vecs = [v.reshape(depth, 1, D_MODEL) for v in (ln_mix_g, ln_mix_b, ln_ffn_g, ln_ffn_b)]
    ffn_in, ffn_down, w_gate, w_proj = (w.astype(BF16) for w in (ffn_w_in, ffn_w_down, ple_w_gate, ple_w_proj))
    for i in range(depth):
        j = i // N_MIXERS
        if i % N_MIXERS == 0:
            q, k, v = _mla_pre(x2, pos2, mla_w_dqkv[j], mla_q_norm[j], mla_kv_norm[j],
                               mla_w_uq[j], mla_w_ukv[j])
            o = _flash_attn(q, k, v, B, S)
            w_o = mla_w_o[j]
        else:
            q, k, lf, v, sg = _hgrn_pre(x2, hgrn_w_in[j], hgrn_lb_logits, i)
            o = _hgrn_rec(q, k, lf, v, sg, hgrn_out_norm[j], B, S)
            w_o = hgrn_w_o[j]
        x2 = _layer_tail(i, o, x2, p3, w_o, vecs[0], vecs[1], ffn_in, ffn_down, vecs[2], vecs[3],
                         w_gate, w_proj)
    return x2.reshape(B, S, D)
```

```python
import functools

import jax
import jax.numpy as jnp
from jax import lax
from jax.experimental import pallas as pl
from jax.experimental.pallas import tpu as pltpu

F32 = jnp.float32
BF16 = jnp.bfloat16

D_MODEL = 1024
DEPTH = 2
N_MIXERS = 2
MLA_HEADS = 8
MLA_Q_LORA = 256
MLA_KV_LORA = 256
MLA_NOPE = 128
MLA_ROPE = 64
MLA_V = 128
ROPE_THETA = 10000.0
HGRN_HEADS = 8
HGRN_DK = D_MODEL // HGRN_HEADS
HGRN_DV = D_MODEL // HGRN_HEADS
D_FF = 2816
D_PLE = 256
LN_EPS = 1e-5
RMS_EPS = 1e-6
DEEPNORM_ALPHA = (2 * DEPTH) ** 0.25

LANES = 128
SUBLANES = 8
VMEM_LIMIT_BYTES = 56 * 1024 * 1024

QK_WIDTH = 2 * LANES
NEG_BIG = -1e30

TOKEN_TILE = 512
TAIL_TILE = 512
TAIL_LEAD_CHUNKS = 1
TAIL_TRAIL_CHUNKS = 1
ATTN_Q_TILE = 2048
ATTN_KV_SUB = 256
V_WIDTH = 2 * LANES
LOG2_E = 1.4426950408889634
HGRN_SEQ_TILE = 1024
HGRN_CHUNK = 128
FFN_CHUNK = 256
HGRN_PRE_COLS = 256
HGRN_PRE_TILE = 512


def _cparams(semantics):
    return pltpu.CompilerParams(dimension_semantics=semantics,
                                vmem_limit_bytes=VMEM_LIMIT_BYTES)


def _const_spec(shape):
    nd = len(shape)
    return pl.BlockSpec(shape, lambda *_: (0,) * nd, pipeline_mode=pl.Buffered(1))


def _head_major_rows(i):
    return (0, i, 0)


def _sigmoid(x):
    return 1.0 / (1.0 + jnp.exp(-x))


def _layer_norm(y, g, b):
    mu = jnp.mean(y, axis=-1, keepdims=True)
    d = y - mu
    var = jnp.mean(d * d, axis=-1, keepdims=True)
    return d * lax.rsqrt(var + LN_EPS) * g + b


def _rms_norm(t, g):
    return t * lax.rsqrt(jnp.mean(t * t, axis=-1, keepdims=True) + RMS_EPS) * g


def _dot(a, b):
    return jnp.dot(a, b, preferred_element_type=F32)


def _dot_nt(a, b):
    return lax.dot_general(a, b, (((1,), (1,)), ((), ())), preferred_element_type=F32)


def _mla_pre_kernel(x_ref, pos_ref, invf_ref, wd_ref, gq_ref, gkv_ref, wq_ref, wkv_ref,
                    q_out, k_out, v_out, *, scale):
    nh = MLA_HEADS
    xb = x_ref[...].astype(BF16)
    down = _dot(xb, wd_ref[...])
    c_q = _rms_norm(down[:, :MLA_Q_LORA], gq_ref[...]).astype(BF16)
    c_kv = _rms_norm(down[:, MLA_Q_LORA:MLA_Q_LORA + MLA_KV_LORA], gkv_ref[...]).astype(BF16)
    ang = pos_ref[...] * invf_ref[...]
    cos = jnp.cos(ang)
    sin = jnp.sin(ang)
    base = MLA_Q_LORA + MLA_KV_LORA
    kr = (down[:, base:base + LANES] * cos + down[:, base + LANES:base + 2 * LANES] * sin).astype(BF16)

    qall = _dot(c_q, wq_ref[...])
    kvall = _dot(c_kv, wkv_ref[...])
    for h in range(nh):
        lo = h * LANES
        qn = qall[:, lo:lo + LANES]
        qr = qall[:, nh * LANES + lo:nh * LANES + lo + LANES]
        qx = qall[:, 2 * nh * LANES + lo:2 * nh * LANES + lo + LANES]
        q_out[h, :, :LANES] = (qn * scale).astype(BF16)
        q_out[h, :, LANES:] = ((qr * cos + qx * sin) * scale).astype(BF16)
        k_out[h, :, :LANES] = kvall[:, lo:lo + LANES].astype(BF16)
        k_out[h, :, LANES:] = kr
        v_out[h, :, :LANES] = kvall[:, nh * LANES + lo:nh * LANES + lo + LANES].astype(BF16)
        v_out[h, :, LANES:] = jnp.ones((x_ref.shape[0], LANES), BF16)


def _rope_slot(w):
    half = MLA_ROPE // 2
    t1, t2 = w[..., :half], w[..., half:]
    z = jnp.zeros(w.shape[:-1] + (LANES - MLA_ROPE,), w.dtype)
    return jnp.concatenate([t1, t2, z], -1), jnp.concatenate([-t2, t1, z], -1)


def _mla_pre(x2, pos2, w_dqkv, q_norm, kv_norm, w_uq, w_ukv):
    T = x2.shape[0]
    tm = min(TOKEN_TILE, T)
    nh = MLA_HEADS
    base = MLA_Q_LORA + MLA_KV_LORA
    slot, rot = _rope_slot(w_dqkv[:, base:])
    wd = jnp.concatenate([w_dqkv[:, :base], slot, rot], axis=1).astype(BF16)
    wq3 = w_uq.reshape(MLA_Q_LORA, nh, MLA_NOPE + MLA_ROPE)
    qslot, qrot = _rope_slot(wq3[:, :, MLA_NOPE:])
    wq = jnp.concatenate([wq3[:, :, :MLA_NOPE].reshape(MLA_Q_LORA, nh * LANES),
                          qslot.reshape(MLA_Q_LORA, nh * LANES),
                          qrot.reshape(MLA_Q_LORA, nh * LANES)], axis=1).astype(BF16)
    wkv3 = w_ukv.reshape(MLA_KV_LORA, nh, MLA_NOPE + MLA_V)
    wkv = jnp.concatenate([wkv3[:, :, :MLA_NOPE].reshape(MLA_KV_LORA, nh * LANES),
                           wkv3[:, :, MLA_NOPE:].reshape(MLA_KV_LORA, nh * LANES)], axis=1).astype(BF16)
    inv_freq = ROPE_THETA ** (-jnp.arange(0, MLA_ROPE, 2, dtype=F32) / MLA_ROPE)
    invf = jnp.concatenate([inv_freq, inv_freq, jnp.zeros((LANES - MLA_ROPE,), F32)])[None, :]
    scale = (MLA_NOPE + MLA_ROPE) ** -0.5 * LOG2_E
    row = lambda i: (i, 0)
    return pl.pallas_call(
        functools.partial(_mla_pre_kernel, scale=scale),
        grid=(T // tm,),
        in_specs=[pl.BlockSpec((tm, D_MODEL), row),
                  pl.BlockSpec((tm, 1), row),
                  _const_spec(invf.shape), _const_spec(wd.shape),
                  _const_spec((1, MLA_Q_LORA)), _const_spec((1, MLA_KV_LORA)),
                  _const_spec(wq.shape), _const_spec(wkv.shape)],
        out_specs=[pl.BlockSpec((nh, tm, QK_WIDTH), _head_major_rows),
                   pl.BlockSpec((nh, tm, QK_WIDTH), _head_major_rows),
                   pl.BlockSpec((nh, tm, V_WIDTH), _head_major_rows)],
        out_shape=[jax.ShapeDtypeStruct((nh, T, QK_WIDTH), BF16),
                   jax.ShapeDtypeStruct((nh, T, QK_WIDTH), BF16),
                   jax.ShapeDtypeStruct((nh, T, V_WIDTH), BF16)],
        compiler_params=_cparams(("parallel",)),
        name="mla_pre",
    )(x2, pos2, invf, wd, q_norm[None, :], kv_norm[None, :], wq, wkv)


def _attn_kernel(q_ref, k_ref, v_ref, o_ref, m_sc, acc_sc, *, tq, sub):
    qi = pl.program_id(2)
    m_sc[...] = jnp.full(m_sc.shape, NEG_BIG, F32)
    acc_sc[...] = jnp.zeros(acc_sc.shape, F32)
    nsub = tq // sub

    def sub_step(kv_start, row0, diag_col0):
        s = _dot_nt(q_ref[0, 0, row0:, :], k_ref[0, 0, pl.ds(kv_start, sub), :])
        if diag_col0 is not None:
            assert row0 == diag_col0
            r = lax.broadcasted_iota(jnp.int32, (sub, sub), 0)
            c = lax.broadcasted_iota(jnp.int32, (sub, sub), 1)
            top = jnp.where(c <= r, s[:sub], NEG_BIG)
            s = top if s.shape[0] == sub else jnp.concatenate([top, s[sub:]], axis=0)
        m_prev = m_sc[row0:, :]
        m_new = jnp.maximum(m_prev, jnp.max(s, axis=1, keepdims=True))
        alpha = jnp.exp2(m_prev - m_new)
        p = jnp.exp2(s - jnp.tile(m_new, (1, sub // LANES))).astype(BF16)
        v = v_ref[0, 0, pl.ds(kv_start, sub), :]
        acc_sc[row0:, :] = jnp.tile(alpha, (1, 2)) * acc_sc[row0:, :] + _dot(p, v)
        m_sc[row0:, :] = m_new

    def body(j, carry):
        base = j * tq
        for c in range(nsub):
            sub_step(pl.multiple_of(base + c * sub, sub), 0, None)
        return carry

    lax.fori_loop(0, qi, body, 0)
    base = qi * tq
    for c in range(nsub):
        sub_step(pl.multiple_of(base + c * sub, sub), c * sub, c * sub)
    acc = acc_sc[...]
    o_ref[0, 0] = (acc[:, :MLA_V] / acc[:, MLA_V:]).astype(o_ref.dtype)


def _flash_attn(q, k, v, B, S):
    nh = MLA_HEADS
    tq = min(ATTN_Q_TILE, S)
    sub = min(ATTN_KV_SUB, tq)
    q4 = q.reshape(nh, B, S, QK_WIDTH)
    k4 = k.reshape(nh, B, S, QK_WIDTH)
    v4 = v.reshape(nh, B, S, V_WIDTH)
    whole_seq = lambda b, h, i: (h, b, 0, 0)
    q_tile = lambda b, h, i: (h, b, i, 0)
    out = pl.pallas_call(
        functools.partial(_attn_kernel, tq=tq, sub=sub),
        grid=(B, nh, S // tq),
        in_specs=[pl.BlockSpec((1, 1, tq, QK_WIDTH), q_tile),
                  pl.BlockSpec((1, 1, S, QK_WIDTH), whole_seq),
                  pl.BlockSpec((1, 1, S, V_WIDTH), whole_seq)],
        out_specs=pl.BlockSpec((1, 1, tq, MLA_V), q_tile),
        out_shape=jax.ShapeDtypeStruct((nh, B, S, MLA_V), BF16),
        scratch_shapes=[pltpu.VMEM((tq, LANES), F32),
                        pltpu.VMEM((tq, V_WIDTH), F32)],
        compiler_params=_cparams(("parallel", "parallel", "arbitrary")),
        name="flash_attn",
    )(q4, k4, v4)
    return out.reshape(nh, B * S, MLA_V)


def _layer_tail_kernel(o0_ref, x0_ref, on_ref, xn_ref, p_ref, wo_ref, g1_ref, b1_ref, win_ref, wdown_ref,
                       g_ref, b_ref, wgate_ref, wproj_ref, out_ref, x1_sc, x1b_sc, ypre_sc):
    def mix(o_ref, x_ref):
        o = jnp.concatenate([o_ref[h] for h in range(o_ref.shape[0])], axis=1)
        return _layer_norm(DEEPNORM_ALPHA * x_ref[...] + _dot(o, wo_ref[...]), g1_ref[...], b1_ref[...])

    def ffn(xb, acc, first, last):
        for c in range(first, last):
            lo = c * FFN_CHUNK
            gate = _dot(xb, win_ref[:, lo:lo + FFN_CHUNK])
            up = _dot(xb, win_ref[:, D_FF + lo:D_FF + lo + FFN_CHUNK])
            act = (gate * _sigmoid(gate) * up).astype(BF16)
            acc = acc + _dot(act, wdown_ref[lo:lo + FFN_CHUNK, :])
        return acc

    @pl.when(pl.program_id(0) == 0)
    def _():
        x1 = mix(o0_ref, x0_ref)
        x1_sc[...] = x1
        x1b_sc[...] = x1.astype(BF16)
        ypre_sc[...] = jnp.zeros(ypre_sc.shape, F32)

    n_chunks = D_FF // FFN_CHUNK
    y = _layer_norm(ypre_sc[...], g_ref[...], b_ref[...])
    xb = x1b_sc[...]
    acc = ffn(xb, jnp.zeros(x1_sc.shape, F32), 0, TAIL_LEAD_CHUNKS)
    gate = _sigmoid(_dot(y.astype(BF16), wgate_ref[...]))
    emb = _dot(p_ref[...].astype(BF16), wproj_ref[...])
    out_ref[...] = y + gate * emb
    acc = ffn(xb, acc, TAIL_LEAD_CHUNKS, n_chunks - TAIL_TRAIL_CHUNKS)
    x1_next = mix(on_ref, xn_ref)
    acc = ffn(xb, acc, n_chunks - TAIL_TRAIL_CHUNKS, n_chunks)
    ypre_sc[...] = DEEPNORM_ALPHA * x1_sc[...] + acc
    x1_sc[...] = x1_next
    x1b_sc[...] = x1_next.astype(BF16)


def _layer_spec(layer, shape):
    nd = len(shape)
    return pl.BlockSpec((None,) + tuple(shape), lambda *_: (layer,) + (0,) * nd,
                        pipeline_mode=pl.Buffered(1))


def _layer_tail(layer, o, x2, p3, w_o, g1, b1, w_in, w_down, g2, b2, w_gate, w_proj):
    T = x2.shape[0]
    tm = min(TAIL_TILE, T)
    nt = T // tm
    nh, _, hw = o.shape
    nxt = lambda s: jnp.minimum(s + 1, nt - 1)
    prv = lambda s: jnp.maximum(s - 1, 0)
    vec = _layer_spec(layer, (1, D_MODEL))
    return pl.pallas_call(
        _layer_tail_kernel,
        grid=(nt + 1,),
        in_specs=[pl.BlockSpec((nh, tm, hw), lambda s: (0, 0, 0), pipeline_mode=pl.Buffered(1)),
                  pl.BlockSpec((tm, D_MODEL), lambda s: (0, 0), pipeline_mode=pl.Buffered(1)),
                  pl.BlockSpec((nh, tm, hw), lambda s: (0, nxt(s), 0)),
                  pl.BlockSpec((tm, D_MODEL), lambda s: (nxt(s), 0)),
                  pl.BlockSpec((None, tm, D_PLE), lambda s: (layer, prv(s), 0)),
                  _const_spec((D_MODEL, D_MODEL)), vec, vec,
                  _layer_spec(layer, (D_MODEL, 2 * D_FF)), _layer_spec(layer, (D_FF, D_MODEL)), vec, vec,
                  _layer_spec(layer, (D_MODEL, D_MODEL)), _layer_spec(layer, (D_PLE, D_MODEL))],
        out_specs=pl.BlockSpec((tm, D_MODEL), lambda s: (prv(s), 0)),
        out_shape=jax.ShapeDtypeStruct((T, D_MODEL), F32),
        scratch_shapes=[pltpu.VMEM((tm, D_MODEL), F32),
                        pltpu.VMEM((tm, D_MODEL), BF16),
                        pltpu.VMEM((tm, D_MODEL), F32)],
        compiler_params=_cparams(("arbitrary",)),
        name="layer_tail",
    )(o, x2, o, x2, p3, w_o.astype(BF16), g1, b1, w_in, w_down, g2, b2, w_gate, w_proj)


def _hgrn_pre_kernel(x_ref, w_ref, lbl_ref, q_out, k_out, lf_out, v_out, sg_out, *, layer):
    n = HGRN_HEADS * HGRN_DK
    logits = lbl_ref[...]
    mx = jnp.max(logits, axis=0, keepdims=True)
    e = jnp.exp(logits - mx)
    soft = e / jnp.sum(e, axis=0, keepdims=True)
    lb = jnp.zeros((1, n), F32)
    for j in range(1, layer + 1):
        lb = lb + soft[j:j + 1, :]
    log_lb = jnp.log(lb)
    log_1m = jnp.log1p(-lb)

    xb = x_ref[...].astype(BF16)
    blocks = [slice(lo_c, lo_c + HGRN_PRE_COLS) for lo_c in range(0, n, HGRN_PRE_COLS)]

    def proj(part, cols):
        return _dot(xb, w_ref[:, part * n + cols.start:part * n + cols.stop])

    def put(out_ref, cols, val):
        for lo_c in range(cols.start, cols.stop, HGRN_DK):
            out_ref[lo_c // HGRN_DK] = val[:, lo_c - cols.start:lo_c - cols.start + HGRN_DK].astype(out_ref.dtype)

    for cols in blocks:
        f = proj(1, cols)
        ef = jnp.exp(-jnp.abs(f))
        r = 1.0 / (1.0 + ef)
        put(k_out, cols, (1.0 - lb[:, cols]) * jnp.where(f >= 0, ef * r, r))
        b = log_1m[:, cols] + (jnp.minimum(f, 0.0) - jnp.log(1.0 + ef))
        a = log_lb[:, cols]
        put(lf_out, cols, jnp.maximum(a, b) + jnp.log(1.0 + jnp.exp(-jnp.abs(a - b))))
    for cols in blocks:
        qp = proj(0, cols)
        put(q_out, cols, qp * _sigmoid(qp))
    for cols in blocks:
        gp = proj(3, cols)
        put(sg_out, cols, gp * _sigmoid(gp))
    for cols in blocks:
        put(v_out, cols, proj(2, cols))


def _hgrn_pre(x2, w_in, lb_logits, layer):
    T = x2.shape[0]
    tm = min(HGRN_PRE_TILE, T)
    n = HGRN_HEADS * HGRN_DK
    row = lambda i: (i, 0)
    outs = [jax.ShapeDtypeStruct((HGRN_HEADS, T, HGRN_DK), dt) for dt in (F32, F32, F32, BF16, F32)]
    return pl.pallas_call(
        functools.partial(_hgrn_pre_kernel, layer=layer),
        grid=(T // tm,),
        in_specs=[pl.BlockSpec((tm, D_MODEL), row), _const_spec((D_MODEL, 4 * n)),
                  _const_spec((DEPTH, n))],
        out_specs=[pl.BlockSpec((HGRN_HEADS, tm, HGRN_DK), _head_major_rows)] * 5,
        out_shape=outs,
        compiler_params=_cparams(("parallel",)),
        name="hgrn_pre",
    )(x2, w_in.astype(BF16), lb_logits)


def _split3(g):
    g1 = g.astype(BF16)
    r1 = g - g1.astype(F32)
    g2 = r1.astype(BF16)
    g3 = (r1 - g2.astype(F32)).astype(BF16)
    return g1, g2, g3


def _level_operand(h, G, g, q, k, g_sc, rowi):
    chunk = G.shape[0]

    def bc(row):
        return jnp.broadcast_to(g_sc[row:row + 1, :], (SUBLANES, LANES))

    if h >= SUBLANES:
        pieces = []
        for r0 in range(0, chunk, SUBLANES):
            ref = bc((r0 // (2 * h)) * (2 * h) + h - 1)
            rows = slice(r0, r0 + SUBLANES)
            if (r0 // h) % 2 == 1:
                pieces.append(q[rows] * jnp.exp2(G[rows] - ref))
            else:
                pieces.append(k[rows] * jnp.exp2(ref - G[rows]))
        return jnp.concatenate(pieces, axis=0)

    up = (rowi & h) != 0
    if h == 1:
        arg = jnp.where(up, g, 0.0)
    else:
        sub = lax.broadcasted_iota(jnp.int32, (SUBLANES, LANES), 0)
        pieces = []
        for r0 in range(0, chunk, SUBLANES):
            ref = bc(r0 + h - 1)
            for start in range(2 * h, SUBLANES, 2 * h):
                ref = jnp.where(sub >= start, bc(r0 + start + h - 1), ref)
            pieces.append(ref)
        d = G - jnp.concatenate(pieces, axis=0)
        arg = jnp.where(up, d, -d)
    return jnp.where(up, q, k) * jnp.exp2(arg)


def _hgrn_rec_kernel(q_ref, k_ref, lf_ref, v_ref, sg_ref, gn_ref, o_ref,
                     st_sc, g_sc, a_sc, qg_sc, kd_sc, *, chunk, n_chunks):
    @pl.when(pl.program_id(2) == 0)
    def _():
        st_sc[...] = jnp.zeros(st_sc.shape, F32)

    ti = lax.broadcasted_iota(jnp.int32, (chunk, chunk), 0)
    si = lax.broadcasted_iota(jnp.int32, (chunk, chunk), 1)
    tri = (si <= ti).astype(BF16)
    lvl = jnp.where(ti >= si, 31 - lax.clz(jnp.bitwise_xor(ti, si)), -2)
    rowi = lax.broadcasted_iota(jnp.int32, (chunk, LANES), 0)
    gn = gn_ref[...]

    def rows_of(c):
        return slice(c * chunk, (c + 1) * chunk)

    def cumulate(c):
        rows = rows_of(c)
        g1, g2, g3 = _split3(lf_ref[0, rows, :] * LOG2_E)
        g_sc[rows, :] = _dot(tri, g1) + _dot(tri, g2) + _dot(tri, g3)

    def decay_matrix(c):
        rows = rows_of(c)
        g = lf_ref[0, rows, :] * LOG2_E
        G = g_sc[rows, :]
        q = q_ref[0, rows, :]
        k = k_ref[0, rows, :]
        g_last = G[chunk - 1:chunk, :]
        qg_sc[rows, :] = (q * jnp.exp2(G)).astype(BF16)
        kd_sc[rows, :] = (k * jnp.exp2(g_last - G)).astype(BF16)
        a = jnp.where(lvl == -1, jnp.sum(q * k, axis=1, keepdims=True), 0.0)
        h = chunk // 2
        while h >= 1:
            z = _level_operand(h, G, g, q, k, g_sc.at[rows, :], rowi).astype(BF16)
            a = jnp.where(lvl == h.bit_length() - 1, _dot_nt(z, z), a)
            h //= 2
        a_sc[rows, :] = a.astype(BF16)

    def advance(c, st):
        rows = rows_of(c)
        v = v_ref[0, rows, :]
        o = _dot_nt(qg_sc[rows, :], st.astype(BF16)) + _dot(a_sc[rows, :], v)
        upd = _dot(v.astype(F32).T.astype(BF16), kd_sc[rows, :])
        st = st * jnp.exp2(g_sc[(c + 1) * chunk - 1:(c + 1) * chunk, :]) + upd
        on = _rms_norm(o, gn)
        o_ref[0, rows, :] = (on * sg_ref[0, rows, :]).astype(o_ref.dtype)
        return st

    st = st_sc[...]
    for step in range(n_chunks + 2):
        if step < n_chunks:
            cumulate(step)
        if 1 <= step <= n_chunks:
            decay_matrix(step - 1)
        if step >= 2:
            st = advance(step - 2, st)
    st_sc[...] = st


def _hgrn_rec(q, k, lf, v, sg, out_norm, B, S):
    nh = HGRN_HEADS
    ts = min(HGRN_SEQ_TILE, S)
    chunk = min(HGRN_CHUNK, ts)
    r4 = lambda t: t.reshape(nh, B, S, HGRN_DK)
    blk = pl.BlockSpec((None, 1, ts, LANES), lambda b, h, i: (h, b, i, 0))
    out = pl.pallas_call(
        functools.partial(_hgrn_rec_kernel, chunk=chunk, n_chunks=ts // chunk),
        grid=(B, nh, S // ts),
        in_specs=[blk, blk, blk, blk, blk, pl.BlockSpec((1, LANES), lambda b, h, i: (0, h))],
        out_specs=blk,
        out_shape=jax.ShapeDtypeStruct((nh, B, S, HGRN_DV), BF16),
        scratch_shapes=[pltpu.VMEM((HGRN_DV, HGRN_DK), F32),
                        pltpu.VMEM((ts, LANES), F32),
                        pltpu.VMEM((ts, chunk), BF16),
                        pltpu.VMEM((ts, LANES), BF16),
                        pltpu.VMEM((ts, LANES), BF16)],
        compiler_params=_cparams(("parallel", "parallel", "arbitrary")),
        name="hgrn_rec",
    )(r4(q), r4(k), r4(lf), r4(v), r4(sg), out_norm[None, :])
    return out.reshape(nh, B * S, HGRN_DV)


def kernel(x, p, positions, mla_w_dqkv, mla_q_norm, mla_kv_norm, mla_w_uq, mla_w_ukv, mla_w_o,
           hgrn_w_in, hgrn_lb_logits, hgrn_out_norm, hgrn_w_o, ffn_w_in, ffn_w_down,
           ln_mix_g, ln_mix_b, ln_ffn_g, ln_ffn_b, ple_w_proj, ple_w_gate):
    B, S, D = x.shape
    T = B * S
    x2 = x.reshape(T, D)
    pos2 = positions.astype(F32).reshape(T, 1)
    depth = p.shape[0]
    p3 = p.reshape(depth, T, D_PLE)
    vecs = [v.reshape(depth, 1, D_MODEL) for v in (ln_mix_g, ln_mix_b, ln_ffn_g, ln_ffn_b)]
    ffn_in, ffn_down, w_gate, w_proj = (w.astype(BF16) for w in (ffn_w_in, ffn_w_down, ple_w_gate, ple_w_proj))
    for i in range(depth):
        j = i // N_MIXERS
        if i % N_MIXERS == 0:
            q, k, v = _mla_pre(x2, pos2, mla_w_dqkv[j], mla_q_norm[j], mla_kv_norm[j],
                               mla_w_uq[j], mla_w_ukv[j])
            o = _flash_attn(q, k, v, B, S)
            w_o = mla_w_o[j]
        else:
            q, k, lf, v, sg = _hgrn_pre(x2, hgrn_w_in[j], hgrn_lb_logits, i)
            o = _hgrn_rec(q, k, lf, v, sg, hgrn_out_norm[j], B, S)
            w_o = hgrn_w_o[j]
        x2 = _layer_tail(i, o, x2, p3, w_o, vecs[0], vecs[1], ffn_in, ffn_down, vecs[2], vecs[3],
                         w_gate, w_proj)
    return x2.reshape(B, S, D)
```

```python
import functools

import jax
import jax.numpy as jnp
from jax import lax
from jax.experimental import pallas as pl
from jax.experimental.pallas import tpu as pltpu

F32 = jnp.float32
BF16 = jnp.bfloat16

D_MODEL = 1024
DEPTH = 2
N_MIXERS = 2
MLA_HEADS = 8
MLA_Q_LORA = 256
MLA_KV_LORA = 256
MLA_NOPE = 128
MLA_ROPE = 64
MLA_V = 128
ROPE_THETA = 10000.0
HGRN_HEADS = 8
HGRN_DK = D_MODEL // HGRN_HEADS
HGRN_DV = D_MODEL // HGRN_HEADS
D_FF = 2816
D_PLE = 256
LN_EPS = 1e-5
RMS_EPS = 1e-6
DEEPNORM_ALPHA = (2 * DEPTH) ** 0.25

LANES = 128
SUBLANES = 8
VMEM_LIMIT_BYTES = 56 * 1024 * 1024

QK_WIDTH = 2 * LANES
NEG_BIG = -1e30

TOKEN_TILE = 512
TAIL_TILE = 512
TAIL_LEAD_CHUNKS = 1
TAIL_TRAIL_CHUNKS = 1
ATTN_Q_TILE = 2048
ATTN_KV_SUB = 256
V_WIDTH = 2 * LANES
LOG2_E = 1.4426950408889634
HGRN_SEQ_TILE = 2048
HGRN_CHUNK = 128
FFN_CHUNK = 256
HGRN_PRE_COLS = 256
HGRN_PRE_TILE = 512


def _cparams(semantics):
    return pltpu.CompilerParams(dimension_semantics=semantics,
                                vmem_limit_bytes=VMEM_LIMIT_BYTES)


def _const_spec(shape):
    nd = len(shape)
    return pl.BlockSpec(shape, lambda *_: (0,) * nd, pipeline_mode=pl.Buffered(1))


def _head_major_rows(i):
    return (0, i, 0)


def _sigmoid(x):
    return 1.0 / (1.0 + jnp.exp(-x))


def _layer_norm(y, g, b):
    mu = jnp.mean(y, axis=-1, keepdims=True)
    d = y - mu
    var = jnp.mean(d * d, axis=-1, keepdims=True)
    return d * lax.rsqrt(var + LN_EPS) * g + b


def _rms_norm(t, g):
    return t * lax.rsqrt(jnp.mean(t * t, axis=-1, keepdims=True) + RMS_EPS) * g


def _dot(a, b):
    return jnp.dot(a, b, preferred_element_type=F32)


def _dot_nt(a, b):
    return lax.dot_general(a, b, (((1,), (1,)), ((), ())), preferred_element_type=F32)


def _mla_pre_kernel(x_ref, pos_ref, invf_ref, wd_ref, gq_ref, gkv_ref, wq_ref, wkv_ref,
                    q_out, k_out, v_out, *, scale):
    nh = MLA_HEADS
    xb = x_ref[...].astype(BF16)
    down = _dot(xb, wd_ref[...])
    c_q = _rms_norm(down[:, :MLA_Q_LORA], gq_ref[...]).astype(BF16)
    c_kv = _rms_norm(down[:, MLA_Q_LORA:MLA_Q_LORA + MLA_KV_LORA], gkv_ref[...]).astype(BF16)
    ang = pos_ref[...] * invf_ref[...]
    cos = jnp.cos(ang)
    sin = jnp.sin(ang)
    base = MLA_Q_LORA + MLA_KV_LORA
    kr = (down[:, base:base + LANES] * cos + down[:, base + LANES:base + 2 * LANES] * sin).astype(BF16)

    qall = _dot(c_q, wq_ref[...])
    kvall = _dot(c_kv, wkv_ref[...])
    for h in range(nh):
        lo = h * LANES
        qn = qall[:, lo:lo + LANES]
        qr = qall[:, nh * LANES + lo:nh * LANES + lo + LANES]
        qx = qall[:, 2 * nh * LANES + lo:2 * nh * LANES + lo + LANES]
        q_out[h, :, :LANES] = (qn * scale).astype(BF16)
        q_out[h, :, LANES:] = ((qr * cos + qx * sin) * scale).astype(BF16)
        k_out[h, :, :LANES] = kvall[:, lo:lo + LANES].astype(BF16)
        k_out[h, :, LANES:] = kr
        v_out[h, :, :LANES] = kvall[:, nh * LANES + lo:nh * LANES + lo + LANES].astype(BF16)
        v_out[h, :, LANES:] = jnp.ones((x_ref.shape[0], LANES), BF16)


def _rope_slot(w):
    half = MLA_ROPE // 2
    t1, t2 = w[..., :half], w[..., half:]
    z = jnp.zeros(w.shape[:-1] + (LANES - MLA_ROPE,), w.dtype)
    return jnp.concatenate([t1, t2, z], -1), jnp.concatenate([-t2, t1, z], -1)


def _mla_pre(x2, pos2, w_dqkv, q_norm, kv_norm, w_uq, w_ukv):
    T = x2.shape[0]
    tm = min(TOKEN_TILE, T)
    nh = MLA_HEADS
    base = MLA_Q_LORA + MLA_KV_LORA
    slot, rot = _rope_slot(w_dqkv[:, base:])
    wd = jnp.concatenate([w_dqkv[:, :base], slot, rot], axis=1).astype(BF16)
    wq3 = w_uq.reshape(MLA_Q_LORA, nh, MLA_NOPE + MLA_ROPE)
    qslot, qrot = _rope_slot(wq3[:, :, MLA_NOPE:])
    wq = jnp.concatenate([wq3[:, :, :MLA_NOPE].reshape(MLA_Q_LORA, nh * LANES),
                          qslot.reshape(MLA_Q_LORA, nh * LANES),
                          qrot.reshape(MLA_Q_LORA, nh * LANES)], axis=1).astype(BF16)
    wkv3 = w_ukv.reshape(MLA_KV_LORA, nh, MLA_NOPE + MLA_V)
    wkv = jnp.concatenate([wkv3[:, :, :MLA_NOPE].reshape(MLA_KV_LORA, nh * LANES),
                           wkv3[:, :, MLA_NOPE:].reshape(MLA_KV_LORA, nh * LANES)], axis=1).astype(BF16)
    inv_freq = ROPE_THETA ** (-jnp.arange(0, MLA_ROPE, 2, dtype=F32) / MLA_ROPE)
    invf = jnp.concatenate([inv_freq, inv_freq, jnp.zeros((LANES - MLA_ROPE,), F32)])[None, :]
    scale = (MLA_NOPE + MLA_ROPE) ** -0.5 * LOG2_E
    row = lambda i: (i, 0)
    return pl.pallas_call(
        functools.partial(_mla_pre_kernel, scale=scale),
        grid=(T // tm,),
        in_specs=[pl.BlockSpec((tm, D_MODEL), row),
                  pl.BlockSpec((tm, 1), row),
                  _const_spec(invf.shape), _const_spec(wd.shape),
                  _const_spec((1, MLA_Q_LORA)), _const_spec((1, MLA_KV_LORA)),
                  _const_spec(wq.shape), _const_spec(wkv.shape)],
        out_specs=[pl.BlockSpec((nh, tm, QK_WIDTH), _head_major_rows),
                   pl.BlockSpec((nh, tm, QK_WIDTH), _head_major_rows),
                   pl.BlockSpec((nh, tm, V_WIDTH), _head_major_rows)],
        out_shape=[jax.ShapeDtypeStruct((nh, T, QK_WIDTH), BF16),
                   jax.ShapeDtypeStruct((nh, T, QK_WIDTH), BF16),
                   jax.ShapeDtypeStruct((nh, T, V_WIDTH), BF16)],
        compiler_params=_cparams(("parallel",)),
        name="mla_pre",
    )(x2, pos2, invf, wd, q_norm[None, :], kv_norm[None, :], wq, wkv)


def _attn_kernel(q_ref, k_ref, v_ref, o_ref, m_sc, acc_sc, *, tq, sub):
    qi = pl.program_id(2)
    m_sc[...] = jnp.full(m_sc.shape, NEG_BIG, F32)
    acc_sc[...] = jnp.zeros(acc_sc.shape, F32)
    nsub = tq // sub

    def sub_step(kv_start, row0, diag_col0):
        s = _dot_nt(q_ref[0, 0, row0:, :], k_ref[0, 0, pl.ds(kv_start, sub), :])
        if diag_col0 is not None:
            assert row0 == diag_col0
            r = lax.broadcasted_iota(jnp.int32, (sub, sub), 0)
            c = lax.broadcasted_iota(jnp.int32, (sub, sub), 1)
            top = jnp.where(c <= r, s[:sub], NEG_BIG)
            s = top if s.shape[0] == sub else jnp.concatenate([top, s[sub:]], axis=0)
        m_prev = m_sc[row0:, :]
        m_new = jnp.maximum(m_prev, jnp.max(s, axis=1, keepdims=True))
        alpha = jnp.exp2(m_prev - m_new)
        p = jnp.exp2(s - jnp.tile(m_new, (1, sub // LANES))).astype(BF16)
        v = v_ref[0, 0, pl.ds(kv_start, sub), :]
        acc_sc[row0:, :] = jnp.tile(alpha, (1, 2)) * acc_sc[row0:, :] + _dot(p, v)
        m_sc[row0:, :] = m_new

    def body(j, carry):
        base = j * tq
        for c in range(nsub):
            sub_step(pl.multiple_of(base + c * sub, sub), 0, None)
        return carry

    lax.fori_loop(0, qi, body, 0)
    base = qi * tq
    for c in range(nsub):
        sub_step(pl.multiple_of(base + c * sub, sub), c * sub, c * sub)
    acc = acc_sc[...]
    o_ref[0, 0] = (acc[:, :MLA_V] / acc[:, MLA_V:]).astype(o_ref.dtype)


def _flash_attn(q, k, v, B, S):
    nh = MLA_HEADS
    tq = min(ATTN_Q_TILE, S)
    sub = min(ATTN_KV_SUB, tq)
    q4 = q.reshape(nh, B, S, QK_WIDTH)
    k4 = k.reshape(nh, B, S, QK_WIDTH)
    v4 = v.reshape(nh, B, S, V_WIDTH)
    whole_seq = lambda b, h, i: (h, b, 0, 0)
    q_tile = lambda b, h, i: (h, b, i, 0)
    out = pl.pallas_call(
        functools.partial(_attn_kernel, tq=tq, sub=sub),
        grid=(B, nh, S // tq),
        in_specs=[pl.BlockSpec((1, 1, tq, QK_WIDTH), q_tile),
                  pl.BlockSpec((1, 1, S, QK_WIDTH), whole_seq),
                  pl.BlockSpec((1, 1, S, V_WIDTH), whole_seq)],
        out_specs=pl.BlockSpec((1, 1, tq, MLA_V), q_tile),
        out_shape=jax.ShapeDtypeStruct((nh, B, S, MLA_V), BF16),
        scratch_shapes=[pltpu.VMEM((tq, LANES), F32),
                        pltpu.VMEM((tq, V_WIDTH), F32)],
        compiler_params=_cparams(("parallel", "parallel", "arbitrary")),
        name="flash_attn",
    )(q4, k4, v4)
    return out.reshape(nh, B * S, MLA_V)


def _layer_tail_kernel(o0_ref, x0_ref, on_ref, xn_ref, p_ref, wo_ref, g1_ref, b1_ref, win_ref, wdown_ref,
                       g_ref, b_ref, wgate_ref, wproj_ref, out_ref, x1_sc, x1b_sc, ypre_sc):
    def mix(o_ref, x_ref):
        o = jnp.concatenate([o_ref[h] for h in range(o_ref.shape[0])], axis=1)
        return _layer_norm(DEEPNORM_ALPHA * x_ref[...] + _dot(o, wo_ref[...]), g1_ref[...], b1_ref[...])

    def ffn(xb, acc, first, last):
        for c in range(first, last):
            lo = c * FFN_CHUNK
            gate = _dot(xb, win_ref[:, lo:lo + FFN_CHUNK])
            up = _dot(xb, win_ref[:, D_FF + lo:D_FF + lo + FFN_CHUNK])
            act = (gate * _sigmoid(gate) * up).astype(BF16)
            acc = acc + _dot(act, wdown_ref[lo:lo + FFN_CHUNK, :])
        return acc

    @pl.when(pl.program_id(0) == 0)
    def _():
        x1 = mix(o0_ref, x0_ref)
        x1_sc[...] = x1
        x1b_sc[...] = x1.astype(BF16)
        ypre_sc[...] = jnp.zeros(ypre_sc.shape, F32)

    n_chunks = D_FF // FFN_CHUNK
    y = _layer_norm(ypre_sc[...], g_ref[...], b_ref[...])
    xb = x1b_sc[...]
    acc = ffn(xb, jnp.zeros(x1_sc.shape, F32), 0, TAIL_LEAD_CHUNKS)
    gate = _sigmoid(_dot(y.astype(BF16), wgate_ref[...]))
    emb = _dot(p_ref[...].astype(BF16), wproj_ref[...])
    out_ref[...] = y + gate * emb
    acc = ffn(xb, acc, TAIL_LEAD_CHUNKS, n_chunks - TAIL_TRAIL_CHUNKS)
    x1_next = mix(on_ref, xn_ref)
    acc = ffn(xb, acc, n_chunks - TAIL_TRAIL_CHUNKS, n_chunks)
    ypre_sc[...] = DEEPNORM_ALPHA * x1_sc[...] + acc
    x1_sc[...] = x1_next
    x1b_sc[...] = x1_next.astype(BF16)


def _layer_spec(layer, shape):
    nd = len(shape)
    return pl.BlockSpec((None,) + tuple(shape), lambda *_: (layer,) + (0,) * nd,
                        pipeline_mode=pl.Buffered(1))


def _layer_tail(layer, o, x2, p3, w_o, g1, b1, w_in, w_down, g2, b2, w_gate, w_proj):
    T = x2.shape[0]
    tm = min(TAIL_TILE, T)
    nt = T // tm
    nh, _, hw = o.shape
    nxt = lambda s: jnp.minimum(s + 1, nt - 1)
    prv = lambda s: jnp.maximum(s - 1, 0)
    vec = _layer_spec(layer, (1, D_MODEL))
    return pl.pallas_call(
        _layer_tail_kernel,
        grid=(nt + 1,),
        in_specs=[pl.BlockSpec((nh, tm, hw), lambda s: (0, 0, 0), pipeline_mode=pl.Buffered(1)),
                  pl.BlockSpec((tm, D_MODEL), lambda s: (0, 0), pipeline_mode=pl.Buffered(1)),
                  pl.BlockSpec((nh, tm, hw), lambda s: (0, nxt(s), 0)),
                  pl.BlockSpec((tm, D_MODEL), lambda s: (nxt(s), 0)),
                  pl.BlockSpec((None, tm, D_PLE), lambda s: (layer, prv(s), 0)),
                  _const_spec((D_MODEL, D_MODEL)), vec, vec,
                  _layer_spec(layer, (D_MODEL, 2 * D_FF)), _layer_spec(layer, (D_FF, D_MODEL)), vec, vec,
                  _layer_spec(layer, (D_MODEL, D_MODEL)), _layer_spec(layer, (D_PLE, D_MODEL))],
        out_specs=pl.BlockSpec((tm, D_MODEL), lambda s: (prv(s), 0)),
        out_shape=jax.ShapeDtypeStruct((T, D_MODEL), F32),
        scratch_shapes=[pltpu.VMEM((tm, D_MODEL), F32),
                        pltpu.VMEM((tm, D_MODEL), BF16),
                        pltpu.VMEM((tm, D_MODEL), F32)],
        compiler_params=_cparams(("arbitrary",)),
        name="layer_tail",
    )(o, x2, o, x2, p3, w_o.astype(BF16), g1, b1, w_in, w_down, g2, b2, w_gate, w_proj)


def _hgrn_pre_kernel(x_ref, w_ref, lbl_ref, q_out, k_out, lf_out, v_out, sg_out, *, layer):
    n = HGRN_HEADS * HGRN_DK
    logits = lbl_ref[...]
    mx = jnp.max(logits, axis=0, keepdims=True)
    e = jnp.exp(logits - mx)
    soft = e / jnp.sum(e, axis=0, keepdims=True)
    lb = jnp.zeros((1, n), F32)
    for j in range(1, layer + 1):
        lb = lb + soft[j:j + 1, :]
    log_lb = jnp.log(lb)
    log_1m = jnp.log1p(-lb)

    xb = x_ref[...].astype(BF16)
    blocks = [slice(lo_c, lo_c + HGRN_PRE_COLS) for lo_c in range(0, n, HGRN_PRE_COLS)]

    def proj(part, cols):
        return _dot(xb, w_ref[:, part * n + cols.start:part * n + cols.stop])

    def put(out_ref, cols, val):
        for lo_c in range(cols.start, cols.stop, HGRN_DK):
            out_ref[lo_c // HGRN_DK] = val[:, lo_c - cols.start:lo_c - cols.start + HGRN_DK].astype(out_ref.dtype)

    for cols in blocks:
        f = proj(1, cols)
        ef = jnp.exp(-jnp.abs(f))
        r = 1.0 / (1.0 + ef)
        put(k_out, cols, (1.0 - lb[:, cols]) * jnp.where(f >= 0, ef * r, r))
        b = log_1m[:, cols] + (jnp.minimum(f, 0.0) - jnp.log(1.0 + ef))
        a = log_lb[:, cols]
        put(lf_out, cols, jnp.maximum(a, b) + jnp.log(1.0 + jnp.exp(-jnp.abs(a - b))))
    for cols in blocks:
        qp = proj(0, cols)
        put(q_out, cols, qp * _sigmoid(qp))
    for cols in blocks:
        gp = proj(3, cols)
        put(sg_out, cols, gp * _sigmoid(gp))
    for cols in blocks:
        put(v_out, cols, proj(2, cols))


def _hgrn_pre(x2, w_in, lb_logits, layer):
    T = x2.shape[0]
    tm = min(HGRN_PRE_TILE, T)
    n = HGRN_HEADS * HGRN_DK
    row = lambda i: (i, 0)
    outs = [jax.ShapeDtypeStruct((HGRN_HEADS, T, HGRN_DK), dt) for dt in (F32, F32, F32, BF16, F32)]
    return pl.pallas_call(
        functools.partial(_hgrn_pre_kernel, layer=layer),
        grid=(T // tm,),
        in_specs=[pl.BlockSpec((tm, D_MODEL), row), _const_spec((D_MODEL, 4 * n)),
                  _const_spec((DEPTH, n))],
        out_specs=[pl.BlockSpec((HGRN_HEADS, tm, HGRN_DK), _head_major_rows)] * 5,
        out_shape=outs,
        compiler_params=_cparams(("parallel",)),
        name="hgrn_pre",
    )(x2, w_in.astype(BF16), lb_logits)


def _split3(g):
    g1 = g.astype(BF16)
    r1 = g - g1.astype(F32)
    g2 = r1.astype(BF16)
    g3 = (r1 - g2.astype(F32)).astype(BF16)
    return g1, g2, g3


def _level_operand(h, G, g, q, k, g_sc, rowi):
    chunk = G.shape[0]

    def bc(row):
        return jnp.broadcast_to(g_sc[row:row + 1, :], (SUBLANES, LANES))

    if h >= SUBLANES:
        pieces = []
        for r0 in range(0, chunk, SUBLANES):
            ref = bc((r0 // (2 * h)) * (2 * h) + h - 1)
            rows = slice(r0, r0 + SUBLANES)
            if (r0 // h) % 2 == 1:
                pieces.append(q[rows] * jnp.exp2(G[rows] - ref))
            else:
                pieces.append(k[rows] * jnp.exp2(ref - G[rows]))
        return jnp.concatenate(pieces, axis=0)

    up = (rowi & h) != 0
    if h == 1:
        arg = jnp.where(up, g, 0.0)
    else:
        sub = lax.broadcasted_iota(jnp.int32, (SUBLANES, LANES), 0)
        pieces = []
        for r0 in range(0, chunk, SUBLANES):
            ref = bc(r0 + h - 1)
            for start in range(2 * h, SUBLANES, 2 * h):
                ref = jnp.where(sub >= start, bc(r0 + start + h - 1), ref)
            pieces.append(ref)
        d = G - jnp.concatenate(pieces, axis=0)
        arg = jnp.where(up, d, -d)
    return jnp.where(up, q, k) * jnp.exp2(arg)


def _hgrn_rec_kernel(q_ref, k_ref, lf_ref, v_ref, sg_ref, gn_ref, o_ref,
                     st_sc, g_sc, a_sc, qg_sc, kd_sc, *, chunk, n_chunks):
    @pl.when(pl.program_id(2) == 0)
    def _():
        st_sc[...] = jnp.zeros(st_sc.shape, F32)

    ti = lax.broadcasted_iota(jnp.int32, (chunk, chunk), 0)
    si = lax.broadcasted_iota(jnp.int32, (chunk, chunk), 1)
    tri = (si <= ti).astype(BF16)
    lvl = jnp.where(ti >= si, 31 - lax.clz(jnp.bitwise_xor(ti, si)), -2)
    rowi = lax.broadcasted_iota(jnp.int32, (chunk, LANES), 0)
    gn = gn_ref[...]

    def rows_of(c):
        return slice(c * chunk, (c + 1) * chunk)

    def cumulate(c):
        rows = rows_of(c)
        g1, g2, g3 = _split3(lf_ref[0, rows, :] * LOG2_E)
        g_sc[rows, :] = _dot(tri, g1) + _dot(tri, g2) + _dot(tri, g3)

    def decay_matrix(c):
        rows = rows_of(c)
        g = lf_ref[0, rows, :] * LOG2_E
        G = g_sc[rows, :]
        q = q_ref[0, rows, :]
        k = k_ref[0, rows, :]
        g_last = G[chunk - 1:chunk, :]
        qg_sc[rows, :] = (q * jnp.exp2(G)).astype(BF16)
        kd_sc[rows, :] = (k * jnp.exp2(g_last - G)).astype(BF16)
        a = jnp.where(lvl == -1, jnp.sum(q * k, axis=1, keepdims=True), 0.0)
        h = chunk // 2
        while h >= 1:
            z32 = _level_operand(h, G, g, q, k, g_sc.at[rows, :], rowi)
            z = z32.astype(BF16)
            level = h.bit_length() - 1
            if h >= SUBLANES:
                groups = range(0, chunk, SUBLANES)
                upper = [r0 for r0 in groups if (r0 // h) % 2 == 1]
                zu = jnp.concatenate([z32[r0:r0 + SUBLANES] for r0 in upper], axis=0).astype(BF16)
                al = _dot_nt(zu, z)
                pieces = []
                for r0 in groups:
                    grp = slice(r0, r0 + SUBLANES)
                    if r0 in upper:
                        u0 = upper.index(r0) * SUBLANES
                        pieces.append(jnp.where(lvl[grp] == level, al[u0:u0 + SUBLANES], a[grp]))
                    else:
                        pieces.append(a[grp])
                a = jnp.concatenate(pieces, axis=0)
            else:
                a = jnp.where(lvl == level, _dot_nt(z, z), a)
            h //= 2
        a_sc[rows, :] = a.astype(BF16)

    def advance(c, st):
        rows = rows_of(c)
        v = v_ref[0, rows, :]
        o = _dot_nt(qg_sc[rows, :], st.astype(BF16)) + _dot(a_sc[rows, :], v)
        upd = _dot(v.astype(F32).T.astype(BF16), kd_sc[rows, :])
        st = st * jnp.exp2(g_sc[(c + 1) * chunk - 1:(c + 1) * chunk, :]) + upd
        on = _rms_norm(o, gn)
        o_ref[0, rows, :] = (on * sg_ref[0, rows, :]).astype(o_ref.dtype)
        return st

    st = st_sc[...]
    for step in range(n_chunks + 2):
        if step < n_chunks:
            cumulate(step)
        if 1 <= step <= n_chunks:
            decay_matrix(step - 1)
        if step >= 2:
            st = advance(step - 2, st)
    st_sc[...] = st


def _hgrn_rec(q, k, lf, v, sg, out_norm, B, S):
    nh = HGRN_HEADS
    ts = min(HGRN_SEQ_TILE, S)
    chunk = min(HGRN_CHUNK, ts)
    r4 = lambda t: t.reshape(nh, B, S, HGRN_DK)
    blk = pl.BlockSpec((None, 1, ts, LANES), lambda b, h, i: (h, b, i, 0))
    out = pl.pallas_call(
        functools.partial(_hgrn_rec_kernel, chunk=chunk, n_chunks=ts // chunk),
        grid=(B, nh, S // ts),
        in_specs=[blk, blk, blk, blk, blk, pl.BlockSpec((1, LANES), lambda b, h, i: (0, h))],
        out_specs=blk,
        out_shape=jax.ShapeDtypeStruct((nh, B, S, HGRN_DV), BF16),
        scratch_shapes=[pltpu.VMEM((HGRN_DV, HGRN_DK), F32),
                        pltpu.VMEM((ts, LANES), F32),
                        pltpu.VMEM((ts, chunk), BF16),
                        pltpu.VMEM((ts, LANES), BF16),
                        pltpu.VMEM((ts, LANES), BF16)],
        compiler_params=_cparams(("parallel", "parallel", "arbitrary")),
        name="hgrn_rec",
    )(r4(q), r4(k), r4(lf), r4(v), r4(sg), out_norm[None, :])
    return out.reshape(nh, B * S, HGRN_DV)


def kernel(x, p, positions, mla_w_dqkv, mla_q_norm, mla_kv_norm, mla_w_uq, mla_w_ukv, mla_w_o,
           hgrn_w_in, hgrn_lb_logits, hgrn_out_norm, hgrn_w_o, ffn_w_in, ffn_w_down,
           ln_mix_g, ln_mix_b, ln_ffn_g, ln_ffn_b, ple_w_proj, ple_w_gate):
    B, S, D = x.shape
    T = B * S
    x2 = x.reshape(T, D)
    pos2 = positions.astype(F32).reshape(T, 1)
    depth = p.shape[0]
    p3 = p.reshape(depth, T, D_PLE)
    vecs = [v.reshape(depth, 1, D_MODEL) for v in (ln_mix_g, ln_mix_b, ln_ffn_g, ln_ffn_b)]
    ffn_in, ffn_down, w_gate, w_proj = (w.astype(BF16) for w in (ffn_w_in, ffn_w_down, ple_w_gate, ple_w_proj))
    for i in range(depth):
        j = i // N_MIXERS
        if i % N_MIXERS == 0:
            q, k, v = _mla_pre(x2, pos2, mla_w_dqkv[j], mla_q_norm[j], mla_kv_norm[j],
                               mla_w_uq[j], mla_w_ukv[j])
            o = _flash_attn(q, k, v, B, S)
            w_o = mla_w_o[j]
        else:
            q, k, lf, v, sg = _hgrn_pre(x2, hgrn_w_in[j], hgrn_lb_logits, i)
            o = _hgrn_rec(q, k, lf, v, sg, hgrn_out_norm[j], B, S)
            w_o = hgrn_w_o[j]
        x2 = _layer_tail(i, o, x2, p3, w_o, vecs[0], vecs[1], ffn_in, ffn_down, vecs[2], vecs[3],
                         w_gate, w_proj)
    return x2.reshape(B, S, D)
```

```python
import functools

import jax
import jax.numpy as jnp
from jax import lax
from jax.experimental import pallas as pl
from jax.experimental.pallas import tpu as pltpu

F32 = jnp.float32
BF16 = jnp.bfloat16

D_MODEL = 1024
DEPTH = 2
N_MIXERS = 2
MLA_HEADS = 8
MLA_Q_LORA = 256
MLA_KV_LORA = 256
MLA_NOPE = 128
MLA_ROPE = 64
MLA_V = 128
ROPE_THETA = 10000.0
HGRN_HEADS = 8
HGRN_DK = D_MODEL // HGRN_HEADS
HGRN_DV = D_MODEL // HGRN_HEADS
D_FF = 2816
D_PLE = 256
LN_EPS = 1e-5
RMS_EPS = 1e-6
DEEPNORM_ALPHA = (2 * DEPTH) ** 0.25

LANES = 128
SUBLANES = 8
VMEM_LIMIT_BYTES = 56 * 1024 * 1024

QK_WIDTH = 2 * LANES
NEG_BIG = -1e30

TOKEN_TILE = 512
TAIL_TILE = 512
TAIL_LEAD_CHUNKS = 1
TAIL_TRAIL_CHUNKS = 1
ATTN_Q_TILE = 2048
ATTN_KV_SUB = 256
V_WIDTH = 2 * LANES
LOG2_E = 1.4426950408889634
HGRN_SEQ_TILE = 2048
HGRN_CHUNK = 128
FFN_CHUNK = 256
HGRN_PRE_COLS = 256
HGRN_PRE_TILE = 512


def _cparams(semantics):
    return pltpu.CompilerParams(dimension_semantics=semantics,
                                vmem_limit_bytes=VMEM_LIMIT_BYTES)


def _const_spec(shape):
    nd = len(shape)
    return pl.BlockSpec(shape, lambda *_: (0,) * nd, pipeline_mode=pl.Buffered(1))


def _head_major_rows(i):
    return (0, i, 0)


def _sigmoid(x):
    return 1.0 / (1.0 + jnp.exp(-x))


def _layer_norm(y, g, b):
    mu = jnp.mean(y, axis=-1, keepdims=True)
    d = y - mu
    var = jnp.mean(d * d, axis=-1, keepdims=True)
    return d * lax.rsqrt(var + LN_EPS) * g + b


def _rms_norm(t, g):
    return t * lax.rsqrt(jnp.mean(t * t, axis=-1, keepdims=True) + RMS_EPS) * g


def _dot(a, b):
    return jnp.dot(a, b, preferred_element_type=F32)


def _dot_nt(a, b):
    return lax.dot_general(a, b, (((1,), (1,)), ((), ())), preferred_element_type=F32)


def _mla_pre_kernel(x0_ref, pos0_ref, xn_ref, posn_ref, invf_ref, wd_ref, gq_ref, gkv_ref, wq_ref, wkv_ref,
                    q_out, k_out, v_out, cq_sc, ckv_sc, kr_sc, cos_sc, sin_sc, *, scale):
    nh = MLA_HEADS

    def front(x_ref, pos_ref):
        down = _dot(x_ref[...].astype(BF16), wd_ref[...])
        cq_sc[...] = _rms_norm(down[:, :MLA_Q_LORA], gq_ref[...]).astype(BF16)
        ckv_sc[...] = _rms_norm(down[:, MLA_Q_LORA:MLA_Q_LORA + MLA_KV_LORA], gkv_ref[...]).astype(BF16)
        ang_t = invf_ref[...] * pos_ref[...]
        cos_t = jnp.cos(ang_t).T
        sin_t = jnp.sin(ang_t).T
        pad = LANES - MLA_ROPE
        cos = jnp.concatenate([cos_t, cos_t, jnp.ones((cos_t.shape[0], pad), F32)], axis=1)
        sin = jnp.concatenate([sin_t, sin_t, jnp.zeros((sin_t.shape[0], pad), F32)], axis=1)
        base = MLA_Q_LORA + MLA_KV_LORA
        kr_sc[...] = (down[:, base:base + LANES] * cos + down[:, base + LANES:base + 2 * LANES] * sin).astype(BF16)
        cos_sc[...] = cos
        sin_sc[...] = sin

    @pl.when(pl.program_id(0) == 0)
    def _():
        front(x0_ref, pos0_ref)

    cos = cos_sc[...]
    sin = sin_sc[...]
    kr = kr_sc[...]
    qall = _dot(cq_sc[...], wq_ref[...])
    kvall = _dot(ckv_sc[...], wkv_ref[...])
    front(xn_ref, posn_ref)
    for h in range(nh):
        lo = h * LANES
        qn = qall[:, lo:lo + LANES]
        qr = qall[:, nh * LANES + lo:nh * LANES + lo + LANES]
        qx = qall[:, 2 * nh * LANES + lo:2 * nh * LANES + lo + LANES]
        q_out[h, :, :LANES] = (qn * scale).astype(BF16)
        q_out[h, :, LANES:] = ((qr * cos + qx * sin) * scale).astype(BF16)
        k_out[h, :, :LANES] = kvall[:, lo:lo + LANES].astype(BF16)
        k_out[h, :, LANES:] = kr
        v_out[h, :, :LANES] = kvall[:, nh * LANES + lo:nh * LANES + lo + LANES].astype(BF16)
        v_out[h, :, LANES:] = jnp.ones((xn_ref.shape[0], LANES), BF16)


def _rope_slot(w):
    half = MLA_ROPE // 2
    t1, t2 = w[..., :half], w[..., half:]
    z = jnp.zeros(w.shape[:-1] + (LANES - MLA_ROPE,), w.dtype)
    return jnp.concatenate([t1, t2, z], -1), jnp.concatenate([-t2, t1, z], -1)


def _mla_pre(x2, pos2, w_dqkv, q_norm, kv_norm, w_uq, w_ukv):
    T = x2.shape[0]
    tm = min(TOKEN_TILE, T)
    nh = MLA_HEADS
    base = MLA_Q_LORA + MLA_KV_LORA
    slot, rot = _rope_slot(w_dqkv[:, base:])
    wd = jnp.concatenate([w_dqkv[:, :base], slot, rot], axis=1).astype(BF16)
    wq3 = w_uq.reshape(MLA_Q_LORA, nh, MLA_NOPE + MLA_ROPE)
    qslot, qrot = _rope_slot(wq3[:, :, MLA_NOPE:])
    wq = jnp.concatenate([wq3[:, :, :MLA_NOPE].reshape(MLA_Q_LORA, nh * LANES),
                          qslot.reshape(MLA_Q_LORA, nh * LANES),
                          qrot.reshape(MLA_Q_LORA, nh * LANES)], axis=1).astype(BF16)
    wkv3 = w_ukv.reshape(MLA_KV_LORA, nh, MLA_NOPE + MLA_V)
    wkv = jnp.concatenate([wkv3[:, :, :MLA_NOPE].reshape(MLA_KV_LORA, nh * LANES),
                           wkv3[:, :, MLA_NOPE:].reshape(MLA_KV_LORA, nh * LANES)], axis=1).astype(BF16)
    inv_freq = ROPE_THETA ** (-jnp.arange(0, MLA_ROPE, 2, dtype=F32) / MLA_ROPE)
    invf = inv_freq[:, None]
    scale = (MLA_NOPE + MLA_ROPE) ** -0.5 * LOG2_E
    nt = T // tm
    first = lambda s: (0, 0)
    nxt = lambda s: (jnp.minimum(s + 1, nt - 1), 0)
    nxt_cols = lambda s: (0, jnp.minimum(s + 1, nt - 1))
    return pl.pallas_call(
        functools.partial(_mla_pre_kernel, scale=scale),
        grid=(nt,),
        in_specs=[pl.BlockSpec((tm, D_MODEL), first, pipeline_mode=pl.Buffered(1)),
                  pl.BlockSpec((1, tm), first, pipeline_mode=pl.Buffered(1)),
                  pl.BlockSpec((tm, D_MODEL), nxt),
                  pl.BlockSpec((1, tm), nxt_cols),
                  _const_spec(invf.shape), _const_spec(wd.shape),
                  _const_spec((1, MLA_Q_LORA)), _const_spec((1, MLA_KV_LORA)),
                  _const_spec(wq.shape), _const_spec(wkv.shape)],
        out_specs=[pl.BlockSpec((nh, tm, QK_WIDTH), _head_major_rows),
                   pl.BlockSpec((nh, tm, QK_WIDTH), _head_major_rows),
                   pl.BlockSpec((nh, tm, V_WIDTH), _head_major_rows)],
        out_shape=[jax.ShapeDtypeStruct((nh, T, QK_WIDTH), BF16),
                   jax.ShapeDtypeStruct((nh, T, QK_WIDTH), BF16),
                   jax.ShapeDtypeStruct((nh, T, V_WIDTH), BF16)],
        scratch_shapes=[pltpu.VMEM((tm, MLA_Q_LORA), BF16),
                        pltpu.VMEM((tm, MLA_KV_LORA), BF16),
                        pltpu.VMEM((tm, LANES), BF16),
                        pltpu.VMEM((tm, LANES), F32),
                        pltpu.VMEM((tm, LANES), F32)],
        compiler_params=_cparams(("arbitrary",)),
        name="mla_pre",
    )(x2, pos2, x2, pos2, invf, wd, q_norm[None, :], kv_norm[None, :], wq, wkv)


def _attn_kernel(q_ref, k_ref, v_ref, o_ref, m_sc, acc_sc, *, tq, sub):
    qi = pl.program_id(2)
    m_sc[...] = jnp.full(m_sc.shape, NEG_BIG, F32)
    acc_sc[...] = jnp.zeros(acc_sc.shape, F32)
    nsub = tq // sub

    def sub_step(kv_start, row0, diag_col0):
        s = _dot_nt(q_ref[0, 0, row0:, :], k_ref[0, 0, pl.ds(kv_start, sub), :])
        if diag_col0 is not None:
            assert row0 == diag_col0
            r = lax.broadcasted_iota(jnp.int32, (sub, sub), 0)
            c = lax.broadcasted_iota(jnp.int32, (sub, sub), 1)
            top = jnp.where(c <= r, s[:sub], NEG_BIG)
            s = top if s.shape[0] == sub else jnp.concatenate([top, s[sub:]], axis=0)
        m_prev = m_sc[row0:, :]
        m_new = jnp.maximum(m_prev, jnp.max(s, axis=1, keepdims=True))
        alpha = jnp.exp2(m_prev - m_new)
        p = jnp.exp2(s - jnp.tile(m_new, (1, sub // LANES))).astype(BF16)
        v = v_ref[0, 0, pl.ds(kv_start, sub), :]
        acc_sc[row0:, :] = jnp.tile(alpha, (1, 2)) * acc_sc[row0:, :] + _dot(p, v)
        m_sc[row0:, :] = m_new

    def body(j, carry):
        base = j * tq
        for c in range(nsub):
            sub_step(pl.multiple_of(base + c * sub, sub), 0, None)
        return carry

    lax.fori_loop(0, qi, body, 0)
    base = qi * tq
    for c in range(nsub):
        sub_step(pl.multiple_of(base + c * sub, sub), c * sub, c * sub)
    acc = acc_sc[...]
    o_ref[0, 0] = (acc[:, :MLA_V] / acc[:, MLA_V:]).astype(o_ref.dtype)


def _flash_attn(q, k, v, B, S):
    nh = MLA_HEADS
    tq = min(ATTN_Q_TILE, S)
    sub = min(ATTN_KV_SUB, tq)
    q4 = q.reshape(nh, B, S, QK_WIDTH)
    k4 = k.reshape(nh, B, S, QK_WIDTH)
    v4 = v.reshape(nh, B, S, V_WIDTH)
    whole_seq = lambda b, h, i: (h, b, 0, 0)
    q_tile = lambda b, h, i: (h, b, i, 0)
    out = pl.pallas_call(
        functools.partial(_attn_kernel, tq=tq, sub=sub),
        grid=(B, nh, S // tq),
        in_specs=[pl.BlockSpec((1, 1, tq, QK_WIDTH), q_tile),
                  pl.BlockSpec((1, 1, S, QK_WIDTH), whole_seq),
                  pl.BlockSpec((1, 1, S, V_WIDTH), whole_seq)],
        out_specs=pl.BlockSpec((1, 1, tq, MLA_V), q_tile),
        out_shape=jax.ShapeDtypeStruct((nh, B, S, MLA_V), BF16),
        scratch_shapes=[pltpu.VMEM((tq, LANES), F32),
                        pltpu.VMEM((tq, V_WIDTH), F32)],
        compiler_params=_cparams(("parallel", "parallel", "arbitrary")),
        name="flash_attn",
    )(q4, k4, v4)
    return out.reshape(nh, B * S, MLA_V)


def _layer_tail_kernel(o0_ref, x0_ref, on_ref, xn_ref, p_ref, wo_ref, g1_ref, b1_ref, win_ref, wdown_ref,
                       g_ref, b_ref, wgate_ref, wproj_ref, out_ref, x1_sc, x1b_sc, ypre_sc):
    def mix(o_ref, x_ref):
        o = jnp.concatenate([o_ref[h] for h in range(o_ref.shape[0])], axis=1)
        return _layer_norm(DEEPNORM_ALPHA * x_ref[...] + _dot(o, wo_ref[...]), g1_ref[...], b1_ref[...])

    def ffn(xb, acc, first, last):
        for c in range(first, last):
            lo = c * FFN_CHUNK
            gate = _dot(xb, win_ref[:, lo:lo + FFN_CHUNK])
            up = _dot(xb, win_ref[:, D_FF + lo:D_FF + lo + FFN_CHUNK])
            act = (gate * _sigmoid(gate) * up).astype(BF16)
            acc = acc + _dot(act, wdown_ref[lo:lo + FFN_CHUNK, :])
        return acc

    @pl.when(pl.program_id(0) == 0)
    def _():
        x1 = mix(o0_ref, x0_ref)
        x1_sc[...] = x1
        x1b_sc[...] = x1.astype(BF16)
        ypre_sc[...] = jnp.zeros(ypre_sc.shape, F32)

    n_chunks = D_FF // FFN_CHUNK
    y = _layer_norm(ypre_sc[...], g_ref[...], b_ref[...])
    xb = x1b_sc[...]
    acc = ffn(xb, jnp.zeros(x1_sc.shape, F32), 0, TAIL_LEAD_CHUNKS)
    gate = _sigmoid(_dot(y.astype(BF16), wgate_ref[...]))
    emb = _dot(p_ref[...].astype(BF16), wproj_ref[...])
    out_ref[...] = y + gate * emb
    acc = ffn(xb, acc, TAIL_LEAD_CHUNKS, n_chunks - TAIL_TRAIL_CHUNKS)
    x1_next = mix(on_ref, xn_ref)
    acc = ffn(xb, acc, n_chunks - TAIL_TRAIL_CHUNKS, n_chunks)
    ypre_sc[...] = DEEPNORM_ALPHA * x1_sc[...] + acc
    x1_sc[...] = x1_next
    x1b_sc[...] = x1_next.astype(BF16)


def _layer_spec(layer, shape):
    nd = len(shape)
    return pl.BlockSpec((None,) + tuple(shape), lambda *_: (layer,) + (0,) * nd,
                        pipeline_mode=pl.Buffered(1))


def _layer_tail(layer, o, x2, p3, w_o, g1, b1, w_in, w_down, g2, b2, w_gate, w_proj):
    T = x2.shape[0]
    tm = min(TAIL_TILE, T)
    nt = T // tm
    nh, _, hw = o.shape
    nxt = lambda s: jnp.minimum(s + 1, nt - 1)
    prv = lambda s: jnp.maximum(s - 1, 0)
    vec = _layer_spec(layer, (1, D_MODEL))
    return pl.pallas_call(
        _layer_tail_kernel,
        grid=(nt + 1,),
        in_specs=[pl.BlockSpec((nh, tm, hw), lambda s: (0, 0, 0), pipeline_mode=pl.Buffered(1)),
                  pl.BlockSpec((tm, D_MODEL), lambda s: (0, 0), pipeline_mode=pl.Buffered(1)),
                  pl.BlockSpec((nh, tm, hw), lambda s: (0, nxt(s), 0)),
                  pl.BlockSpec((tm, D_MODEL), lambda s: (nxt(s), 0)),
                  pl.BlockSpec((None, tm, D_PLE), lambda s: (layer, prv(s), 0)),
                  _const_spec((D_MODEL, D_MODEL)), vec, vec,
                  _layer_spec(layer, (D_MODEL, 2 * D_FF)), _layer_spec(layer, (D_FF, D_MODEL)), vec, vec,
                  _layer_spec(layer, (D_MODEL, D_MODEL)), _layer_spec(layer, (D_PLE, D_MODEL))],
        out_specs=pl.BlockSpec((tm, D_MODEL), lambda s: (prv(s), 0)),
        out_shape=jax.ShapeDtypeStruct((T, D_MODEL), F32),
        scratch_shapes=[pltpu.VMEM((tm, D_MODEL), F32),
                        pltpu.VMEM((tm, D_MODEL), BF16),
                        pltpu.VMEM((tm, D_MODEL), F32)],
        compiler_params=_cparams(("arbitrary",)),
        name="layer_tail",
    )(o, x2, o, x2, p3, w_o.astype(BF16), g1, b1, w_in, w_down, g2, b2, w_gate, w_proj)


def _hgrn_pre_kernel(x_ref, w_ref, lbl_ref, q_out, k_out, lf_out, v_out, sg_out, *, layer):
    n = HGRN_HEADS * HGRN_DK
    logits = lbl_ref[...]
    mx = jnp.max(logits, axis=0, keepdims=True)
    e = jnp.exp(logits - mx)
    soft = e / jnp.sum(e, axis=0, keepdims=True)
    lb = jnp.zeros((1, n), F32)
    for j in range(1, layer + 1):
        lb = lb + soft[j:j + 1, :]
    log_lb = jnp.log(lb)
    log_1m = jnp.log1p(-lb)

    xb = x_ref[...].astype(BF16)
    blocks = [slice(lo_c, lo_c + HGRN_PRE_COLS) for lo_c in range(0, n, HGRN_PRE_COLS)]

    def proj(part, cols):
        return _dot(xb, w_ref[:, part * n + cols.start:part * n + cols.stop])

    def put(out_ref, cols, val):
        for lo_c in range(cols.start, cols.stop, HGRN_DK):
            out_ref[lo_c // HGRN_DK] = val[:, lo_c - cols.start:lo_c - cols.start + HGRN_DK].astype(out_ref.dtype)

    for cols in blocks:
        f = proj(1, cols)
        ef = jnp.exp(-jnp.abs(f))
        r = 1.0 / (1.0 + ef)
        put(k_out, cols, (1.0 - lb[:, cols]) * jnp.where(f >= 0, ef * r, r))
        b = log_1m[:, cols] + (jnp.minimum(f, 0.0) - jnp.log(1.0 + ef))
        a = log_lb[:, cols]
        put(lf_out, cols, jnp.maximum(a, b) + jnp.log(1.0 + jnp.exp(-jnp.abs(a - b))))
    for cols in blocks:
        qp = proj(0, cols)
        put(q_out, cols, qp * _sigmoid(qp))
    for cols in blocks:
        gp = proj(3, cols)
        put(sg_out, cols, gp * _sigmoid(gp))
    for cols in blocks:
        put(v_out, cols, proj(2, cols))


def _hgrn_pre(x2, w_in, lb_logits, layer):
    T = x2.shape[0]
    tm = min(HGRN_PRE_TILE, T)
    n = HGRN_HEADS * HGRN_DK
    row = lambda i: (i, 0)
    outs = [jax.ShapeDtypeStruct((HGRN_HEADS, T, HGRN_DK), dt) for dt in (F32, F32, F32, BF16, F32)]
    return pl.pallas_call(
        functools.partial(_hgrn_pre_kernel, layer=layer),
        grid=(T // tm,),
        in_specs=[pl.BlockSpec((tm, D_MODEL), row), _const_spec((D_MODEL, 4 * n)),
                  _const_spec((DEPTH, n))],
        out_specs=[pl.BlockSpec((HGRN_HEADS, tm, HGRN_DK), _head_major_rows)] * 5,
        out_shape=outs,
        compiler_params=_cparams(("parallel",)),
        name="hgrn_pre",
    )(x2, w_in.astype(BF16), lb_logits)


def _split3(g):
    g1 = g.astype(BF16)
    r1 = g - g1.astype(F32)
    g2 = r1.astype(BF16)
    g3 = (r1 - g2.astype(F32)).astype(BF16)
    return g1, g2, g3


def _level_operand(h, G, g, q, k, g_sc, rowi):
    chunk = G.shape[0]

    def bc(row):
        return jnp.broadcast_to(g_sc[row:row + 1, :], (SUBLANES, LANES))

    if h >= SUBLANES:
        pieces = []
        for r0 in range(0, chunk, SUBLANES):
            ref = bc((r0 // (2 * h)) * (2 * h) + h - 1)
            rows = slice(r0, r0 + SUBLANES)
            if (r0 // h) % 2 == 1:
                pieces.append(q[rows] * jnp.exp2(G[rows] - ref))
            else:
                pieces.append(k[rows] * jnp.exp2(ref - G[rows]))
        return jnp.concatenate(pieces, axis=0)

    up = (rowi & h) != 0
    if h == 1:
        arg = jnp.where(up, g, 0.0)
    else:
        sub = lax.broadcasted_iota(jnp.int32, (SUBLANES, LANES), 0)
        pieces = []
        for r0 in range(0, chunk, SUBLANES):
            ref = bc(r0 + h - 1)
            for start in range(2 * h, SUBLANES, 2 * h):
                ref = jnp.where(sub >= start, bc(r0 + start + h - 1), ref)
            pieces.append(ref)
        d = G - jnp.concatenate(pieces, axis=0)
        arg = jnp.where(up, d, -d)
    return jnp.where(up, q, k) * jnp.exp2(arg)


def _hgrn_rec_kernel(q_ref, k_ref, lf_ref, v_ref, sg_ref, gn_ref, o_ref,
                     st_sc, g_sc, a_sc, qg_sc, kd_sc, *, chunk, n_chunks):
    @pl.when(pl.program_id(2) == 0)
    def _():
        st_sc[...] = jnp.zeros(st_sc.shape, F32)

    ti = lax.broadcasted_iota(jnp.int32, (chunk, chunk), 0)
    si = lax.broadcasted_iota(jnp.int32, (chunk, chunk), 1)
    tri = (si <= ti).astype(BF16)
    lvl = jnp.where(ti >= si, 31 - lax.clz(jnp.bitwise_xor(ti, si)), -2)
    rowi = lax.broadcasted_iota(jnp.int32, (chunk, LANES), 0)
    gn = gn_ref[...]

    def rows_of(c):
        return slice(c * chunk, (c + 1) * chunk)

    def cumulate(c):
        rows = rows_of(c)
        g1, g2, g3 = _split3(lf_ref[0, rows, :] * LOG2_E)
        g_sc[rows, :] = _dot(tri, g1) + _dot(tri, g2) + _dot(tri, g3)

    def decay_matrix(c):
        rows = rows_of(c)
        g = lf_ref[0, rows, :] * LOG2_E
        G = g_sc[rows, :]
        q = q_ref[0, rows, :]
        k = k_ref[0, rows, :]
        g_last = G[chunk - 1:chunk, :]
        qg_sc[rows, :] = (q * jnp.exp2(G)).astype(BF16)
        kd_sc[rows, :] = (k * jnp.exp2(g_last - G)).astype(BF16)
        a = jnp.where(lvl == -1, jnp.sum(q * k, axis=1, keepdims=True), 0.0)
        h = chunk // 2
        while h >= 1:
            z32 = _level_operand(h, G, g, q, k, g_sc.at[rows, :], rowi)
            z = z32.astype(BF16)
            level = h.bit_length() - 1
            if h >= SUBLANES:
                groups = range(0, chunk, SUBLANES)
                upper = [r0 for r0 in groups if (r0 // h) % 2 == 1]
                zu = jnp.concatenate([z32[r0:r0 + SUBLANES] for r0 in upper], axis=0).astype(BF16)
                al = _dot_nt(zu, z)
                pieces = []
                for r0 in groups:
                    grp = slice(r0, r0 + SUBLANES)
                    if r0 in upper:
                        u0 = upper.index(r0) * SUBLANES
                        pieces.append(jnp.where(lvl[grp] == level, al[u0:u0 + SUBLANES], a[grp]))
                    else:
                        pieces.append(a[grp])
                a = jnp.concatenate(pieces, axis=0)
            else:
                a = jnp.where(lvl == level, _dot_nt(z, z), a)
            h //= 2
        a_sc[rows, :] = a.astype(BF16)

    def advance(c, st):
        rows = rows_of(c)
        v = v_ref[0, rows, :]
        o = _dot_nt(qg_sc[rows, :], st.astype(BF16)) + _dot(a_sc[rows, :], v)
        upd = _dot(v.astype(F32).T.astype(BF16), kd_sc[rows, :])
        st = st * jnp.exp2(g_sc[(c + 1) * chunk - 1:(c + 1) * chunk, :]) + upd
        on = _rms_norm(o, gn)
        o_ref[0, rows, :] = (on * sg_ref[0, rows, :]).astype(o_ref.dtype)
        return st

    st = st_sc[...]
    for step in range(n_chunks + 2):
        if step < n_chunks:
            cumulate(step)
        if 1 <= step <= n_chunks:
            decay_matrix(step - 1)
        if step >= 2:
            st = advance(step - 2, st)
    st_sc[...] = st


def _hgrn_rec(q, k, lf, v, sg, out_norm, B, S):
    nh = HGRN_HEADS
    ts = min(HGRN_SEQ_TILE, S)
    chunk = min(HGRN_CHUNK, ts)
    r4 = lambda t: t.reshape(nh, B, S, HGRN_DK)
    blk = pl.BlockSpec((None, 1, ts, LANES), lambda b, h, i: (h, b, i, 0))
    out = pl.pallas_call(
        functools.partial(_hgrn_rec_kernel, chunk=chunk, n_chunks=ts // chunk),
        grid=(B, nh, S // ts),
        in_specs=[blk, blk, blk, blk, blk, pl.BlockSpec((1, LANES), lambda b, h, i: (0, h))],
        out_specs=blk,
        out_shape=jax.ShapeDtypeStruct((nh, B, S, HGRN_DV), BF16),
        scratch_shapes=[pltpu.VMEM((HGRN_DV, HGRN_DK), F32),
                        pltpu.VMEM((ts, LANES), F32),
                        pltpu.VMEM((ts, chunk), BF16),
                        pltpu.VMEM((ts, LANES), BF16),
                        pltpu.VMEM((ts, LANES), BF16)],
        compiler_params=_cparams(("parallel", "parallel", "arbitrary")),
        name="hgrn_rec",
    )(r4(q), r4(k), r4(lf), r4(v), r4(sg), out_norm[None, :])
    return out.reshape(nh, B * S, HGRN_DV)


def kernel(x, p, positions, mla_w_dqkv, mla_q_norm, mla_kv_norm, mla_w_uq, mla_w_ukv, mla_w_o,
           hgrn_w_in, hgrn_lb_logits, hgrn_out_norm, hgrn_w_o, ffn_w_in, ffn_w_down,
           ln_mix_g, ln_mix_b, ln_ffn_g, ln_ffn_b, ple_w_proj, ple_w_gate):
    B, S, D = x.shape
    T = B * S
    x2 = x.reshape(T, D)
    pos2 = positions.astype(F32).reshape(1, T)
    depth = p.shape[0]
    p3 = p.reshape(depth, T, D_PLE)
    vecs = [v.reshape(depth, 1, D_MODEL) for v in (ln_mix_g, ln_mix_b, ln_ffn_g, ln_ffn_b)]
    ffn_in, ffn_down, w_gate, w_proj = (w.astype(BF16) for w in (ffn_w_in, ffn_w_down, ple_w_gate, ple_w_proj))
    for i in range(depth):
        j = i // N_MIXERS
        if i % N_MIXERS == 0:
            q, k, v = _mla_pre(x2, pos2, mla_w_dqkv[j], mla_q_norm[j], mla_kv_norm[j],
                               mla_w_uq[j], mla_w_ukv[j])
            o = _flash_attn(q, k, v, B, S)
            w_o = mla_w_o[j]
        else:
            q, k, lf, v, sg = _hgrn_pre(x2, hgrn_w_in[j], hgrn_lb_logits, i)
            o = _hgrn_rec(q, k, lf, v, sg, hgrn_out_norm[j], B, S)
            w_o = hgrn_w_o[j]
        x2 = _layer_tail(i, o, x2, p3, w_o, vecs[0], vecs[1], ffn_in, ffn_down, vecs[2], vecs[3],
                         w_gate, w_proj)
    return x2.reshape(B, S, D)
```

```python
import functools

import jax
import jax.numpy as jnp
from jax import lax
from jax.experimental import pallas as pl
from jax.experimental.pallas import tpu as pltpu

F32 = jnp.float32
BF16 = jnp.bfloat16

D_MODEL = 1024
DEPTH = 2
N_MIXERS = 2
MLA_HEADS = 8
MLA_Q_LORA = 256
MLA_KV_LORA = 256
MLA_NOPE = 128
MLA_ROPE = 64
MLA_V = 128
ROPE_THETA = 10000.0
HGRN_HEADS = 8
HGRN_DK = D_MODEL // HGRN_HEADS
HGRN_DV = D_MODEL // HGRN_HEADS
D_FF = 2816
D_PLE = 256
LN_EPS = 1e-5
RMS_EPS = 1e-6
DEEPNORM_ALPHA = (2 * DEPTH) ** 0.25

LANES = 128
SUBLANES = 8
VMEM_LIMIT_BYTES = 56 * 1024 * 1024

QK_WIDTH = 2 * LANES
NEG_BIG = -1e30

TOKEN_TILE = 512
TAIL_TILE = 512
TAIL_LEAD_CHUNKS = 1
TAIL_TRAIL_CHUNKS = 1
ATTN_Q_TILE = 4096
ATTN_KV_SUB = 256
V_WIDTH = 2 * LANES
LOG2_E = 1.4426950408889634
HGRN_SEQ_TILE = 4096
HGRN_CHUNK = 128
FFN_CHUNK = 256
HGRN_PRE_COLS = 256
HGRN_PRE_TILE = 512


def _cparams(semantics):
    return pltpu.CompilerParams(dimension_semantics=semantics,
                                vmem_limit_bytes=VMEM_LIMIT_BYTES)


def _const_spec(shape):
    nd = len(shape)
    return pl.BlockSpec(shape, lambda *_: (0,) * nd, pipeline_mode=pl.Buffered(1))


def _head_major_rows(i):
    return (0, i, 0)


def _sigmoid(x):
    return 1.0 / (1.0 + jnp.exp(-x))


def _layer_norm(y, g, b):
    mu = jnp.mean(y, axis=-1, keepdims=True)
    d = y - mu
    var = jnp.mean(d * d, axis=-1, keepdims=True)
    return d * lax.rsqrt(var + LN_EPS) * g + b


def _rms_norm(t, g):
    return t * lax.rsqrt(jnp.mean(t * t, axis=-1, keepdims=True) + RMS_EPS) * g


def _dot(a, b):
    return jnp.dot(a, b, preferred_element_type=F32)


def _dot_nt(a, b):
    return lax.dot_general(a, b, (((1,), (1,)), ((), ())), preferred_element_type=F32)


def _mla_pre_kernel(x0_ref, pos0_ref, xn_ref, posn_ref, invf_ref, wd_ref, gq_ref, gkv_ref, wq_ref, wkv_ref,
                    q_out, k_out, v_out, cq_sc, ckv_sc, kr_sc, cos_sc, sin_sc, *, scale):
    nh = MLA_HEADS

    def front(x_ref, pos_ref):
        down = _dot(x_ref[...].astype(BF16), wd_ref[...])
        cq_sc[...] = _rms_norm(down[:, :MLA_Q_LORA], gq_ref[...]).astype(BF16)
        ckv_sc[...] = _rms_norm(down[:, MLA_Q_LORA:MLA_Q_LORA + MLA_KV_LORA], gkv_ref[...]).astype(BF16)
        ang_t = invf_ref[...] * pos_ref[...]
        cos_t = jnp.cos(ang_t).T
        sin_t = jnp.sin(ang_t).T
        pad = LANES - MLA_ROPE
        cos = jnp.concatenate([cos_t, cos_t, jnp.ones((cos_t.shape[0], pad), F32)], axis=1)
        sin = jnp.concatenate([sin_t, sin_t, jnp.zeros((sin_t.shape[0], pad), F32)], axis=1)
        base = MLA_Q_LORA + MLA_KV_LORA
        kr_sc[...] = (down[:, base:base + LANES] * cos + down[:, base + LANES:base + 2 * LANES] * sin).astype(BF16)
        cos_sc[...] = cos
        sin_sc[...] = sin

    @pl.when(pl.program_id(0) == 0)
    def _():
        front(x0_ref, pos0_ref)

    cos = cos_sc[...]
    sin = sin_sc[...]
    kr = kr_sc[...]
    qall = _dot(cq_sc[...], wq_ref[...])
    kvall = _dot(ckv_sc[...], wkv_ref[...])
    front(xn_ref, posn_ref)
    for h in range(nh):
        lo = h * LANES
        qn = qall[:, lo:lo + LANES]
        qr = qall[:, nh * LANES + lo:nh * LANES + lo + LANES]
        qx = qall[:, 2 * nh * LANES + lo:2 * nh * LANES + lo + LANES]
        q_out[h, :, :LANES] = (qn * scale).astype(BF16)
        q_out[h, :, LANES:] = ((qr * cos + qx * sin) * scale).astype(BF16)
        k_out[h, :, :LANES] = kvall[:, lo:lo + LANES].astype(BF16)
        k_out[h, :, LANES:] = kr
        v_out[h, :, :LANES] = kvall[:, nh * LANES + lo:nh * LANES + lo + LANES].astype(BF16)
        v_out[h, :, LANES:] = jnp.ones((xn_ref.shape[0], LANES), BF16)


def _rope_slot(w):
    half = MLA_ROPE // 2
    t1, t2 = w[..., :half], w[..., half:]
    z = jnp.zeros(w.shape[:-1] + (LANES - MLA_ROPE,), w.dtype)
    return jnp.concatenate([t1, t2, z], -1), jnp.concatenate([-t2, t1, z], -1)


def _mla_pre(x2, pos2, w_dqkv, q_norm, kv_norm, w_uq, w_ukv):
    T = x2.shape[0]
    tm = min(TOKEN_TILE, T)
    nh = MLA_HEADS
    base = MLA_Q_LORA + MLA_KV_LORA
    slot, rot = _rope_slot(w_dqkv[:, base:])
    wd = jnp.concatenate([w_dqkv[:, :base], slot, rot], axis=1).astype(BF16)
    wq3 = w_uq.reshape(MLA_Q_LORA, nh, MLA_NOPE + MLA_ROPE)
    qslot, qrot = _rope_slot(wq3[:, :, MLA_NOPE:])
    wq = jnp.concatenate([wq3[:, :, :MLA_NOPE].reshape(MLA_Q_LORA, nh * LANES),
                          qslot.reshape(MLA_Q_LORA, nh * LANES),
                          qrot.reshape(MLA_Q_LORA, nh * LANES)], axis=1).astype(BF16)
    wkv3 = w_ukv.reshape(MLA_KV_LORA, nh, MLA_NOPE + MLA_V)
    wkv = jnp.concatenate([wkv3[:, :, :MLA_NOPE].reshape(MLA_KV_LORA, nh * LANES),
                           wkv3[:, :, MLA_NOPE:].reshape(MLA_KV_LORA, nh * LANES)], axis=1).astype(BF16)
    inv_freq = ROPE_THETA ** (-jnp.arange(0, MLA_ROPE, 2, dtype=F32) / MLA_ROPE)
    invf = inv_freq[:, None]
    scale = (MLA_NOPE + MLA_ROPE) ** -0.5 * LOG2_E
    nt = T // tm
    first = lambda s: (0, 0)
    nxt = lambda s: (jnp.minimum(s + 1, nt - 1), 0)
    nxt_cols = lambda s: (0, jnp.minimum(s + 1, nt - 1))
    return pl.pallas_call(
        functools.partial(_mla_pre_kernel, scale=scale),
        grid=(nt,),
        in_specs=[pl.BlockSpec((tm, D_MODEL), first, pipeline_mode=pl.Buffered(1)),
                  pl.BlockSpec((1, tm), first, pipeline_mode=pl.Buffered(1)),
                  pl.BlockSpec((tm, D_MODEL), nxt),
                  pl.BlockSpec((1, tm), nxt_cols),
                  _const_spec(invf.shape), _const_spec(wd.shape),
                  _const_spec((1, MLA_Q_LORA)), _const_spec((1, MLA_KV_LORA)),
                  _const_spec(wq.shape), _const_spec(wkv.shape)],
        out_specs=[pl.BlockSpec((nh, tm, QK_WIDTH), _head_major_rows),
                   pl.BlockSpec((nh, tm, QK_WIDTH), _head_major_rows),
                   pl.BlockSpec((nh, tm, V_WIDTH), _head_major_rows)],
        out_shape=[jax.ShapeDtypeStruct((nh, T, QK_WIDTH), BF16),
                   jax.ShapeDtypeStruct((nh, T, QK_WIDTH), BF16),
                   jax.ShapeDtypeStruct((nh, T, V_WIDTH), BF16)],
        scratch_shapes=[pltpu.VMEM((tm, MLA_Q_LORA), BF16),
                        pltpu.VMEM((tm, MLA_KV_LORA), BF16),
                        pltpu.VMEM((tm, LANES), BF16),
                        pltpu.VMEM((tm, LANES), F32),
                        pltpu.VMEM((tm, LANES), F32)],
        compiler_params=_cparams(("arbitrary",)),
        name="mla_pre",
    )(x2, pos2, x2, pos2, invf, wd, q_norm[None, :], kv_norm[None, :], wq, wkv)


def _attn_kernel(q_ref, k_ref, v_ref, o_ref, m_sc, acc_sc, *, tq, sub):
    qi = pl.program_id(2)
    m_sc[...] = jnp.full(m_sc.shape, NEG_BIG, F32)
    acc_sc[...] = jnp.zeros(acc_sc.shape, F32)
    nsub = tq // sub

    def sub_step(kv_start, row0, diag_col0):
        s = _dot_nt(q_ref[0, 0, row0:, :], k_ref[0, 0, pl.ds(kv_start, sub), :])
        if diag_col0 is not None:
            assert row0 == diag_col0
            r = lax.broadcasted_iota(jnp.int32, (sub, sub), 0)
            c = lax.broadcasted_iota(jnp.int32, (sub, sub), 1)
            top = jnp.where(c <= r, s[:sub], NEG_BIG)
            s = top if s.shape[0] == sub else jnp.concatenate([top, s[sub:]], axis=0)
        m_prev = m_sc[row0:, :]
        m_new = jnp.maximum(m_prev, jnp.max(s, axis=1, keepdims=True))
        alpha = jnp.exp2(m_prev - m_new)
        p = jnp.exp2(s - jnp.tile(m_new, (1, sub // LANES))).astype(BF16)
        v = v_ref[0, 0, pl.ds(kv_start, sub), :]
        acc_sc[row0:, :] = jnp.tile(alpha, (1, 2)) * acc_sc[row0:, :] + _dot(p, v)
        m_sc[row0:, :] = m_new

    def body(j, carry):
        base = j * tq
        for c in range(nsub):
            sub_step(pl.multiple_of(base + c * sub, sub), 0, None)
        return carry

    lax.fori_loop(0, qi, body, 0)
    base = qi * tq
    for c in range(nsub):
        sub_step(pl.multiple_of(base + c * sub, sub), c * sub, c * sub)
    acc = acc_sc[...]
    o_ref[0, 0] = (acc[:, :MLA_V] / acc[:, MLA_V:]).astype(o_ref.dtype)


def _flash_attn(q, k, v, B, S):
    nh = MLA_HEADS
    tq = min(ATTN_Q_TILE, S)
    sub = min(ATTN_KV_SUB, tq)
    q4 = q.reshape(nh, B, S, QK_WIDTH)
    k4 = k.reshape(nh, B, S, QK_WIDTH)
    v4 = v.reshape(nh, B, S, V_WIDTH)
    whole_seq = lambda b, h, i: (h, b, 0, 0)
    q_tile = lambda b, h, i: (h, b, i, 0)
    out = pl.pallas_call(
        functools.partial(_attn_kernel, tq=tq, sub=sub),
        grid=(B, nh, S // tq),
        in_specs=[pl.BlockSpec((1, 1, tq, QK_WIDTH), q_tile),
                  pl.BlockSpec((1, 1, S, QK_WIDTH), whole_seq),
                  pl.BlockSpec((1, 1, S, V_WIDTH), whole_seq)],
        out_specs=pl.BlockSpec((1, 1, tq, MLA_V), q_tile),
        out_shape=jax.ShapeDtypeStruct((nh, B, S, MLA_V), BF16),
        scratch_shapes=[pltpu.VMEM((tq, LANES), F32),
                        pltpu.VMEM((tq, V_WIDTH), F32)],
        compiler_params=_cparams(("parallel", "parallel", "arbitrary")),
        name="flash_attn",
    )(q4, k4, v4)
    return out.reshape(nh, B * S, MLA_V)


def _layer_tail_kernel(o0_ref, x0_ref, on_ref, xn_ref, p_ref, wo_ref, g1_ref, b1_ref, win_ref, wdown_ref,
                       g_ref, b_ref, wgate_ref, wproj_ref, out_ref, x1_sc, x1b_sc, ypre_sc):
    def mix(o_ref, x_ref):
        o = jnp.concatenate([o_ref[h] for h in range(o_ref.shape[0])], axis=1)
        return _layer_norm(DEEPNORM_ALPHA * x_ref[...] + _dot(o, wo_ref[...]), g1_ref[...], b1_ref[...])

    def ffn(xb, acc, first, last):
        for c in range(first, last):
            lo = c * FFN_CHUNK
            gate = _dot(xb, win_ref[:, lo:lo + FFN_CHUNK])
            up = _dot(xb, win_ref[:, D_FF + lo:D_FF + lo + FFN_CHUNK])
            act = (gate * _sigmoid(gate) * up).astype(BF16)
            acc = acc + _dot(act, wdown_ref[lo:lo + FFN_CHUNK, :])
        return acc

    @pl.when(pl.program_id(0) == 0)
    def _():
        x1 = mix(o0_ref, x0_ref)
        x1_sc[...] = x1
        x1b_sc[...] = x1.astype(BF16)
        ypre_sc[...] = jnp.zeros(ypre_sc.shape, F32)

    n_chunks = D_FF // FFN_CHUNK
    y = _layer_norm(ypre_sc[...], g_ref[...], b_ref[...])
    xb = x1b_sc[...]
    acc = ffn(xb, jnp.zeros(x1_sc.shape, F32), 0, TAIL_LEAD_CHUNKS)
    gate = _sigmoid(_dot(y.astype(BF16), wgate_ref[...]))
    emb = _dot(p_ref[...].astype(BF16), wproj_ref[...])
    out_ref[...] = y + gate * emb
    acc = ffn(xb, acc, TAIL_LEAD_CHUNKS, n_chunks - TAIL_TRAIL_CHUNKS)
    x1_next = mix(on_ref, xn_ref)
    acc = ffn(xb, acc, n_chunks - TAIL_TRAIL_CHUNKS, n_chunks)
    ypre_sc[...] = DEEPNORM_ALPHA * x1_sc[...] + acc
    x1_sc[...] = x1_next
    x1b_sc[...] = x1_next.astype(BF16)


def _layer_spec(layer, shape):
    nd = len(shape)
    return pl.BlockSpec((None,) + tuple(shape), lambda *_: (layer,) + (0,) * nd,
                        pipeline_mode=pl.Buffered(1))


def _layer_tail(layer, o, x2, p3, w_o, g1, b1, w_in, w_down, g2, b2, w_gate, w_proj):
    T = x2.shape[0]
    tm = min(TAIL_TILE, T)
    nt = T // tm
    nh, _, hw = o.shape
    nxt = lambda s: jnp.minimum(s + 1, nt - 1)
    prv = lambda s: jnp.maximum(s - 1, 0)
    vec = _layer_spec(layer, (1, D_MODEL))
    return pl.pallas_call(
        _layer_tail_kernel,
        grid=(nt + 1,),
        in_specs=[pl.BlockSpec((nh, tm, hw), lambda s: (0, 0, 0), pipeline_mode=pl.Buffered(1)),
                  pl.BlockSpec((tm, D_MODEL), lambda s: (0, 0), pipeline_mode=pl.Buffered(1)),
                  pl.BlockSpec((nh, tm, hw), lambda s: (0, nxt(s), 0)),
                  pl.BlockSpec((tm, D_MODEL), lambda s: (nxt(s), 0)),
                  pl.BlockSpec((None, tm, D_PLE), lambda s: (layer, prv(s), 0)),
                  _const_spec((D_MODEL, D_MODEL)), vec, vec,
                  _layer_spec(layer, (D_MODEL, 2 * D_FF)), _layer_spec(layer, (D_FF, D_MODEL)), vec, vec,
                  _layer_spec(layer, (D_MODEL, D_MODEL)), _layer_spec(layer, (D_PLE, D_MODEL))],
        out_specs=pl.BlockSpec((tm, D_MODEL), lambda s: (prv(s), 0)),
        out_shape=jax.ShapeDtypeStruct((T, D_MODEL), F32),
        scratch_shapes=[pltpu.VMEM((tm, D_MODEL), F32),
                        pltpu.VMEM((tm, D_MODEL), BF16),
                        pltpu.VMEM((tm, D_MODEL), F32)],
        compiler_params=_cparams(("arbitrary",)),
        name="layer_tail",
    )(o, x2, o, x2, p3, w_o.astype(BF16), g1, b1, w_in, w_down, g2, b2, w_gate, w_proj)


def _hgrn_pre_kernel(x_ref, w_ref, lbl_ref, q_out, k_out, lf_out, v_out, sg_out, *, layer):
    n = HGRN_HEADS * HGRN_DK
    logits = lbl_ref[...]
    mx = jnp.max(logits, axis=0, keepdims=True)
    e = jnp.exp(logits - mx)
    soft = e / jnp.sum(e, axis=0, keepdims=True)
    lb = jnp.zeros((1, n), F32)
    for j in range(1, layer + 1):
        lb = lb + soft[j:j + 1, :]
    log_lb = jnp.log(lb)
    log_1m = jnp.log1p(-lb)

    xb = x_ref[...].astype(BF16)
    blocks = [slice(lo_c, lo_c + HGRN_PRE_COLS) for lo_c in range(0, n, HGRN_PRE_COLS)]

    def proj(part, cols):
        return _dot(xb, w_ref[:, part * n + cols.start:part * n + cols.stop])

    def put(out_ref, cols, val):
        for lo_c in range(cols.start, cols.stop, HGRN_DK):
            out_ref[lo_c // HGRN_DK] = val[:, lo_c - cols.start:lo_c - cols.start + HGRN_DK].astype(out_ref.dtype)

    for cols in blocks:
        f = proj(1, cols)
        ef = jnp.exp(-jnp.abs(f))
        r = 1.0 / (1.0 + ef)
        put(k_out, cols, (1.0 - lb[:, cols]) * jnp.where(f >= 0, ef * r, r))
        b = log_1m[:, cols] + (jnp.minimum(f, 0.0) - jnp.log(1.0 + ef))
        a = log_lb[:, cols]
        put(lf_out, cols, jnp.maximum(a, b) + jnp.log(1.0 + jnp.exp(-jnp.abs(a - b))))
    for cols in blocks:
        qp = proj(0, cols)
        put(q_out, cols, qp * _sigmoid(qp))
    for cols in blocks:
        gp = proj(3, cols)
        put(sg_out, cols, gp * _sigmoid(gp))
    for cols in blocks:
        put(v_out, cols, proj(2, cols))


def _hgrn_pre(x2, w_in, lb_logits, layer):
    T = x2.shape[0]
    tm = min(HGRN_PRE_TILE, T)
    n = HGRN_HEADS * HGRN_DK
    row = lambda i: (i, 0)
    outs = [jax.ShapeDtypeStruct((HGRN_HEADS, T, HGRN_DK), dt) for dt in (F32, F32, F32, BF16, F32)]
    return pl.pallas_call(
        functools.partial(_hgrn_pre_kernel, layer=layer),
        grid=(T // tm,),
        in_specs=[pl.BlockSpec((tm, D_MODEL), row), _const_spec((D_MODEL, 4 * n)),
                  _const_spec((DEPTH, n))],
        out_specs=[pl.BlockSpec((HGRN_HEADS, tm, HGRN_DK), _head_major_rows)] * 5,
        out_shape=outs,
        compiler_params=_cparams(("parallel",)),
        name="hgrn_pre",
    )(x2, w_in.astype(BF16), lb_logits)


def _split3(g):
    g1 = g.astype(BF16)
    r1 = g - g1.astype(F32)
    g2 = r1.astype(BF16)
    g3 = (r1 - g2.astype(F32)).astype(BF16)
    return g1, g2, g3


def _level_operand(h, G, g, q, k, g_sc, rowi):
    chunk = G.shape[0]

    def bc(row):
        return jnp.broadcast_to(g_sc[row:row + 1, :], (SUBLANES, LANES))

    if h >= SUBLANES:
        pieces = []
        for r0 in range(0, chunk, SUBLANES):
            ref = bc((r0 // (2 * h)) * (2 * h) + h - 1)
            rows = slice(r0, r0 + SUBLANES)
            if (r0 // h) % 2 == 1:
                pieces.append(q[rows] * jnp.exp2(G[rows] - ref))
            else:
                pieces.append(k[rows] * jnp.exp2(ref - G[rows]))
        return jnp.concatenate(pieces, axis=0)

    up = (rowi & h) != 0
    if h == 1:
        arg = jnp.where(up, g, 0.0)
    else:
        sub = lax.broadcasted_iota(jnp.int32, (SUBLANES, LANES), 0)
        pieces = []
        for r0 in range(0, chunk, SUBLANES):
            ref = bc(r0 + h - 1)
            for start in range(2 * h, SUBLANES, 2 * h):
                ref = jnp.where(sub >= start, bc(r0 + start + h - 1), ref)
            pieces.append(ref)
        d = G - jnp.concatenate(pieces, axis=0)
        arg = jnp.where(up, d, -d)
    return jnp.where(up, q, k) * jnp.exp2(arg)


def _hgrn_rec_kernel(q_ref, k_ref, lf_ref, v_ref, sg_ref, gn_ref, o_ref,
                     st_sc, g_sc, a_sc, qg_sc, kd_sc, *, chunk, n_chunks):
    @pl.when(pl.program_id(2) == 0)
    def _():
        st_sc[...] = jnp.zeros(st_sc.shape, F32)

    ti = lax.broadcasted_iota(jnp.int32, (chunk, chunk), 0)
    si = lax.broadcasted_iota(jnp.int32, (chunk, chunk), 1)
    tri = (si <= ti).astype(BF16)
    lvl = jnp.where(ti >= si, 31 - lax.clz(jnp.bitwise_xor(ti, si)), -2)
    rowi = lax.broadcasted_iota(jnp.int32, (chunk, LANES), 0)
    gn = gn_ref[...]

    def rows_of(c):
        return slice(c * chunk, (c + 1) * chunk)

    def cumulate(c):
        rows = rows_of(c)
        g1, g2, g3 = _split3(lf_ref[0, rows, :] * LOG2_E)
        g_sc[rows, :] = _dot(tri, g1) + _dot(tri, g2) + _dot(tri, g3)

    def decay_matrix(c):
        rows = rows_of(c)
        g = lf_ref[0, rows, :] * LOG2_E
        G = g_sc[rows, :]
        q = q_ref[0, rows, :]
        k = k_ref[0, rows, :]
        g_last = G[chunk - 1:chunk, :]
        qg_sc[rows, :] = (q * jnp.exp2(G)).astype(BF16)
        kd_sc[rows, :] = (k * jnp.exp2(g_last - G)).astype(BF16)
        a = jnp.where(lvl == -1, jnp.sum(q * k, axis=1, keepdims=True), 0.0)
        h = chunk // 2
        while h >= 1:
            z32 = _level_operand(h, G, g, q, k, g_sc.at[rows, :], rowi)
            z = z32.astype(BF16)
            level = h.bit_length() - 1
            if h >= SUBLANES:
                groups = range(0, chunk, SUBLANES)
                upper = [r0 for r0 in groups if (r0 // h) % 2 == 1]
                zu = jnp.concatenate([z32[r0:r0 + SUBLANES] for r0 in upper], axis=0).astype(BF16)
                al = _dot_nt(zu, z)
                pieces = []
                for r0 in groups:
                    grp = slice(r0, r0 + SUBLANES)
                    if r0 in upper:
                        u0 = upper.index(r0) * SUBLANES
                        pieces.append(jnp.where(lvl[grp] == level, al[u0:u0 + SUBLANES], a[grp]))
                    else:
                        pieces.append(a[grp])
                a = jnp.concatenate(pieces, axis=0)
            else:
                a = jnp.where(lvl == level, _dot_nt(z, z), a)
            h //= 2
        a_sc[rows, :] = a.astype(BF16)

    def advance(c, st):
        rows = rows_of(c)
        v = v_ref[0, rows, :]
        o = _dot_nt(qg_sc[rows, :], st.astype(BF16)) + _dot(a_sc[rows, :], v)
        upd = _dot(v.astype(F32).T.astype(BF16), kd_sc[rows, :])
        st = st * jnp.exp2(g_sc[(c + 1) * chunk - 1:(c + 1) * chunk, :]) + upd
        on = _rms_norm(o, gn)
        o_ref[0, rows, :] = (on * sg_ref[0, rows, :]).astype(o_ref.dtype)
        return st

    st = st_sc[...]
    for step in range(n_chunks + 2):
        if step < n_chunks:
            cumulate(step)
        if 1 <= step <= n_chunks:
            decay_matrix(step - 1)
        if step >= 2:
            st = advance(step - 2, st)
    st_sc[...] = st


def _hgrn_rec(q, k, lf, v, sg, out_norm, B, S):
    nh = HGRN_HEADS
    ts = min(HGRN_SEQ_TILE, S)
    chunk = min(HGRN_CHUNK, ts)
    r4 = lambda t: t.reshape(nh, B, S, HGRN_DK)
    blk = pl.BlockSpec((None, 1, ts, LANES), lambda b, h, i: (h, b, i, 0))
    out = pl.pallas_call(
        functools.partial(_hgrn_rec_kernel, chunk=chunk, n_chunks=ts // chunk),
        grid=(B, nh, S // ts),
        in_specs=[blk, blk, blk, blk, blk, pl.BlockSpec((1, LANES), lambda b, h, i: (0, h))],
        out_specs=blk,
        out_shape=jax.ShapeDtypeStruct((nh, B, S, HGRN_DV), BF16),
        scratch_shapes=[pltpu.VMEM((HGRN_DV, HGRN_DK), F32),
                        pltpu.VMEM((ts, LANES), F32),
                        pltpu.VMEM((ts, chunk), BF16),
                        pltpu.VMEM((ts, LANES), BF16),
                        pltpu.VMEM((ts, LANES), BF16)],
        compiler_params=_cparams(("parallel", "parallel", "arbitrary")),
        name="hgrn_rec",
    )(r4(q), r4(k), r4(lf), r4(v), r4(sg), out_norm[None, :])
    return out.reshape(nh, B * S, HGRN_DV)


def kernel(x, p, positions, mla_w_dqkv, mla_q_norm, mla_kv_norm, mla_w_uq, mla_w_ukv, mla_w_o,
           hgrn_w_in, hgrn_lb_logits, hgrn_out_norm, hgrn_w_o, ffn_w_in, ffn_w_down,
           ln_mix_g, ln_mix_b, ln_ffn_g, ln_ffn_b, ple_w_proj, ple_w_gate):
    B, S, D = x.shape
    T = B * S
    x2 = x.reshape(T, D)
    pos2 = positions.astype(F32).reshape(1, T)
    depth = p.shape[0]
    p3 = p.reshape(depth, T, D_PLE)
    vecs = [v.reshape(depth, 1, D_MODEL) for v in (ln_mix_g, ln_mix_b, ln_ffn_g, ln_ffn_b)]
    ffn_in, ffn_down, w_gate, w_proj = (w.astype(BF16) for w in (ffn_w_in, ffn_w_down, ple_w_gate, ple_w_proj))
    for i in range(depth):
        j = i // N_MIXERS
        if i % N_MIXERS == 0:
            q, k, v = _mla_pre(x2, pos2, mla_w_dqkv[j], mla_q_norm[j], mla_kv_norm[j],
                               mla_w_uq[j], mla_w_ukv[j])
            o = _flash_attn(q, k, v, B, S)
            w_o = mla_w_o[j]
        else:
            q, k, lf, v, sg = _hgrn_pre(x2, hgrn_w_in[j], hgrn_lb_logits, i)
            o = _hgrn_rec(q, k, lf, v, sg, hgrn_out_norm[j], B, S)
            w_o = hgrn_w_o[j]
        x2 = _layer_tail(i, o, x2, p3, w_o, vecs[0], vecs[1], ffn_in, ffn_down, vecs[2], vecs[3],
                         w_gate, w_proj)
    return x2.reshape(B, S, D)
```

```python
import functools

import jax
import jax.numpy as jnp
from jax import lax
from jax.experimental import pallas as pl
from jax.experimental.pallas import tpu as pltpu

F32 = jnp.float32
BF16 = jnp.bfloat16

D_MODEL = 1024
DEPTH = 2
N_MIXERS = 2
MLA_HEADS = 8
MLA_Q_LORA = 256
MLA_KV_LORA = 256
MLA_NOPE = 128
MLA_ROPE = 64
MLA_V = 128
ROPE_THETA = 10000.0
HGRN_HEADS = 8
HGRN_DK = D_MODEL // HGRN_HEADS
HGRN_DV = D_MODEL // HGRN_HEADS
D_FF = 2816
D_PLE = 256
LN_EPS = 1e-5
RMS_EPS = 1e-6
DEEPNORM_ALPHA = (2 * DEPTH) ** 0.25

LANES = 128
SUBLANES = 8
VMEM_LIMIT_BYTES = 56 * 1024 * 1024

QK_WIDTH = 2 * LANES
NEG_BIG = -1e30

TOKEN_TILE = 512
TAIL_TILE = 512
TAIL_LEAD_CHUNKS = 2
TAIL_TRAIL_CHUNKS = 2
ATTN_Q_TILE = 4096
ATTN_KV_SUB = 256
V_WIDTH = 2 * LANES
LOG2_E = 1.4426950408889634
HGRN_SEQ_TILE = 4096
HGRN_CHUNK = 128
FFN_CHUNK = 256
HGRN_PRE_COLS = 256
HGRN_PRE_TILE = 512


def _cparams(semantics):
    return pltpu.CompilerParams(dimension_semantics=semantics,
                                vmem_limit_bytes=VMEM_LIMIT_BYTES)


def _const_spec(shape):
    nd = len(shape)
    return pl.BlockSpec(shape, lambda *_: (0,) * nd, pipeline_mode=pl.Buffered(1))


def _head_major_rows(i):
    return (0, i, 0)


def _sigmoid(x):
    return 1.0 / (1.0 + jnp.exp(-x))


def _layer_norm(y, g, b):
    mu = jnp.mean(y, axis=-1, keepdims=True)
    d = y - mu
    var = jnp.mean(d * d, axis=-1, keepdims=True)
    return d * lax.rsqrt(var + LN_EPS) * g + b


def _rms_norm(t, g):
    return t * lax.rsqrt(jnp.mean(t * t, axis=-1, keepdims=True) + RMS_EPS) * g


def _dot(a, b):
    return jnp.dot(a, b, preferred_element_type=F32)


def _dot_nt(a, b):
    return lax.dot_general(a, b, (((1,), (1,)), ((), ())), preferred_element_type=F32)


def _mla_pre_kernel(x0_ref, pos0_ref, xn_ref, posn_ref, invf_ref, wd_ref, gq_ref, gkv_ref, wq_ref, wkv_ref,
                    q_out, k_out, v_out, cq_sc, ckv_sc, kr_sc, cos_sc, sin_sc, *, scale):
    nh = MLA_HEADS

    def front(x_ref, pos_ref):
        down = _dot(x_ref[...].astype(BF16), wd_ref[...])
        cq_sc[...] = _rms_norm(down[:, :MLA_Q_LORA], gq_ref[...]).astype(BF16)
        ckv_sc[...] = _rms_norm(down[:, MLA_Q_LORA:MLA_Q_LORA + MLA_KV_LORA], gkv_ref[...]).astype(BF16)
        ang_t = invf_ref[...] * pos_ref[...]
        cos_t = jnp.cos(ang_t).T
        sin_t = jnp.sin(ang_t).T
        pad = LANES - MLA_ROPE
        cos = jnp.concatenate([cos_t, cos_t, jnp.ones((cos_t.shape[0], pad), F32)], axis=1)
        sin = jnp.concatenate([sin_t, sin_t, jnp.zeros((sin_t.shape[0], pad), F32)], axis=1)
        base = MLA_Q_LORA + MLA_KV_LORA
        kr_sc[...] = (down[:, base:base + LANES] * cos + down[:, base + LANES:base + 2 * LANES] * sin).astype(BF16)
        cos_sc[...] = cos
        sin_sc[...] = sin

    @pl.when(pl.program_id(0) == 0)
    def _():
        front(x0_ref, pos0_ref)

    cos = cos_sc[...]
    sin = sin_sc[...]
    kr = kr_sc[...]
    qall = _dot(cq_sc[...], wq_ref[...])
    kvall = _dot(ckv_sc[...], wkv_ref[...])
    front(xn_ref, posn_ref)
    for h in range(nh):
        lo = h * LANES
        qn = qall[:, lo:lo + LANES]
        qr = qall[:, nh * LANES + lo:nh * LANES + lo + LANES]
        qx = qall[:, 2 * nh * LANES + lo:2 * nh * LANES + lo + LANES]
        q_out[h, :, :LANES] = (qn * scale).astype(BF16)
        q_out[h, :, LANES:] = ((qr * cos + qx * sin) * scale).astype(BF16)
        k_out[h, :, :LANES] = kvall[:, lo:lo + LANES].astype(BF16)
        k_out[h, :, LANES:] = kr
        v_out[h, :, :LANES] = kvall[:, nh * LANES + lo:nh * LANES + lo + LANES].astype(BF16)
        v_out[h, :, LANES:] = jnp.ones((xn_ref.shape[0], LANES), BF16)


def _rope_slot(w):
    half = MLA_ROPE // 2
    t1, t2 = w[..., :half], w[..., half:]
    z = jnp.zeros(w.shape[:-1] + (LANES - MLA_ROPE,), w.dtype)
    return jnp.concatenate([t1, t2, z], -1), jnp.concatenate([-t2, t1, z], -1)


def _mla_pre(x2, pos2, w_dqkv, q_norm, kv_norm, w_uq, w_ukv):
    T = x2.shape[0]
    tm = min(TOKEN_TILE, T)
    nh = MLA_HEADS
    base = MLA_Q_LORA + MLA_KV_LORA
    slot, rot = _rope_slot(w_dqkv[:, base:])
    wd = jnp.concatenate([w_dqkv[:, :base], slot, rot], axis=1).astype(BF16)
    wq3 = w_uq.reshape(MLA_Q_LORA, nh, MLA_NOPE + MLA_ROPE)
    qslot, qrot = _rope_slot(wq3[:, :, MLA_NOPE:])
    wq = jnp.concatenate([wq3[:, :, :MLA_NOPE].reshape(MLA_Q_LORA, nh * LANES),
                          qslot.reshape(MLA_Q_LORA, nh * LANES),
                          qrot.reshape(MLA_Q_LORA, nh * LANES)], axis=1).astype(BF16)
    wkv3 = w_ukv.reshape(MLA_KV_LORA, nh, MLA_NOPE + MLA_V)
    wkv = jnp.concatenate([wkv3[:, :, :MLA_NOPE].reshape(MLA_KV_LORA, nh * LANES),
                           wkv3[:, :, MLA_NOPE:].reshape(MLA_KV_LORA, nh * LANES)], axis=1).astype(BF16)
    inv_freq = ROPE_THETA ** (-jnp.arange(0, MLA_ROPE, 2, dtype=F32) / MLA_ROPE)
    invf = inv_freq[:, None]
    scale = (MLA_NOPE + MLA_ROPE) ** -0.5 * LOG2_E
    nt = T // tm
    first = lambda s: (0, 0)
    nxt = lambda s: (jnp.minimum(s + 1, nt - 1), 0)
    nxt_cols = lambda s: (0, jnp.minimum(s + 1, nt - 1))
    return pl.pallas_call(
        functools.partial(_mla_pre_kernel, scale=scale),
        grid=(nt,),
        in_specs=[pl.BlockSpec((tm, D_MODEL), first, pipeline_mode=pl.Buffered(1)),
                  pl.BlockSpec((1, tm), first, pipeline_mode=pl.Buffered(1)),
                  pl.BlockSpec((tm, D_MODEL), nxt),
                  pl.BlockSpec((1, tm), nxt_cols),
                  _const_spec(invf.shape), _const_spec(wd.shape),
                  _const_spec((1, MLA_Q_LORA)), _const_spec((1, MLA_KV_LORA)),
                  _const_spec(wq.shape), _const_spec(wkv.shape)],
        out_specs=[pl.BlockSpec((nh, tm, QK_WIDTH), _head_major_rows),
                   pl.BlockSpec((nh, tm, QK_WIDTH), _head_major_rows),
                   pl.BlockSpec((nh, tm, V_WIDTH), _head_major_rows)],
        out_shape=[jax.ShapeDtypeStruct((nh, T, QK_WIDTH), BF16),
                   jax.ShapeDtypeStruct((nh, T, QK_WIDTH), BF16),
                   jax.ShapeDtypeStruct((nh, T, V_WIDTH), BF16)],
        scratch_shapes=[pltpu.VMEM((tm, MLA_Q_LORA), BF16),
                        pltpu.VMEM((tm, MLA_KV_LORA), BF16),
                        pltpu.VMEM((tm, LANES), BF16),
                        pltpu.VMEM((tm, LANES), F32),
                        pltpu.VMEM((tm, LANES), F32)],
        compiler_params=_cparams(("arbitrary",)),
        name="mla_pre",
    )(x2, pos2, x2, pos2, invf, wd, q_norm[None, :], kv_norm[None, :], wq, wkv)


def _attn_kernel(q_ref, k_ref, v_ref, o_ref, m_sc, acc_sc, *, tq, sub):
    qi = pl.program_id(2)
    m_sc[...] = jnp.full(m_sc.shape, NEG_BIG, F32)
    acc_sc[...] = jnp.zeros(acc_sc.shape, F32)
    nsub = tq // sub

    def sub_step(kv_start, row0, diag_col0):
        s = _dot_nt(q_ref[0, 0, row0:, :], k_ref[0, 0, pl.ds(kv_start, sub), :])
        if diag_col0 is not None:
            assert row0 == diag_col0
            r = lax.broadcasted_iota(jnp.int32, (sub, sub), 0)
            c = lax.broadcasted_iota(jnp.int32, (sub, sub), 1)
            top = jnp.where(c <= r, s[:sub], NEG_BIG)
            s = top if s.shape[0] == sub else jnp.concatenate([top, s[sub:]], axis=0)
        m_prev = m_sc[row0:, :]
        m_new = jnp.maximum(m_prev, jnp.max(s, axis=1, keepdims=True))
        alpha = jnp.exp2(m_prev - m_new)
        p = jnp.exp2(s - jnp.tile(m_new, (1, sub // LANES))).astype(BF16)
        v = v_ref[0, 0, pl.ds(kv_start, sub), :]
        acc_sc[row0:, :] = jnp.tile(alpha, (1, 2)) * acc_sc[row0:, :] + _dot(p, v)
        m_sc[row0:, :] = m_new

    def body(j, carry):
        base = j * tq
        for c in range(nsub):
            sub_step(pl.multiple_of(base + c * sub, sub), 0, None)
        return carry

    lax.fori_loop(0, qi, body, 0)
    base = qi * tq
    for c in range(nsub):
        sub_step(pl.multiple_of(base + c * sub, sub), c * sub, c * sub)
    acc = acc_sc[...]
    o_ref[0, 0] = (acc[:, :MLA_V] / acc[:, MLA_V:]).astype(o_ref.dtype)


def _flash_attn(q, k, v, B, S):
    nh = MLA_HEADS
    tq = min(ATTN_Q_TILE, S)
    sub = min(ATTN_KV_SUB, tq)
    q4 = q.reshape(nh, B, S, QK_WIDTH)
    k4 = k.reshape(nh, B, S, QK_WIDTH)
    v4 = v.reshape(nh, B, S, V_WIDTH)
    whole_seq = lambda b, h, i: (h, b, 0, 0)
    q_tile = lambda b, h, i: (h, b, i, 0)
    out = pl.pallas_call(
        functools.partial(_attn_kernel, tq=tq, sub=sub),
        grid=(B, nh, S // tq),
        in_specs=[pl.BlockSpec((1, 1, tq, QK_WIDTH), q_tile),
                  pl.BlockSpec((1, 1, S, QK_WIDTH), whole_seq),
                  pl.BlockSpec((1, 1, S, V_WIDTH), whole_seq)],
        out_specs=pl.BlockSpec((1, 1, tq, MLA_V), q_tile),
        out_shape=jax.ShapeDtypeStruct((nh, B, S, MLA_V), BF16),
        scratch_shapes=[pltpu.VMEM((tq, LANES), F32),
                        pltpu.VMEM((tq, V_WIDTH), F32)],
        compiler_params=_cparams(("parallel", "parallel", "arbitrary")),
        name="flash_attn",
    )(q4, k4, v4)
    return out.reshape(nh, B * S, MLA_V)


def _layer_tail_kernel(o0_ref, x0_ref, on_ref, xn_ref, p_ref, wo_ref, g1_ref, b1_ref, win_ref, wdown_ref,
                       g_ref, b_ref, wgate_ref, wproj_ref, out_ref, x1_sc, x1b_sc, ypre_sc):
    def mix(o_ref, x_ref):
        o = jnp.concatenate([o_ref[h] for h in range(o_ref.shape[0])], axis=1)
        return _layer_norm(DEEPNORM_ALPHA * x_ref[...] + _dot(o, wo_ref[...]), g1_ref[...], b1_ref[...])

    def ffn(xb, acc, first, last):
        for c in range(first, last):
            lo = c * FFN_CHUNK
            gate = _dot(xb, win_ref[:, lo:lo + FFN_CHUNK])
            up = _dot(xb, win_ref[:, D_FF + lo:D_FF + lo + FFN_CHUNK])
            act = (gate * _sigmoid(gate) * up).astype(BF16)
            acc = acc + _dot(act, wdown_ref[lo:lo + FFN_CHUNK, :])
        return acc

    @pl.when(pl.program_id(0) == 0)
    def _():
        x1 = mix(o0_ref, x0_ref)
        x1_sc[...] = x1
        x1b_sc[...] = x1.astype(BF16)
        ypre_sc[...] = jnp.zeros(ypre_sc.shape, F32)

    n_chunks = D_FF // FFN_CHUNK
    y = _layer_norm(ypre_sc[...], g_ref[...], b_ref[...])
    xb = x1b_sc[...]
    acc = ffn(xb, jnp.zeros(x1_sc.shape, F32), 0, TAIL_LEAD_CHUNKS)
    gate = _sigmoid(_dot(y.astype(BF16), wgate_ref[...]))
    emb = _dot(p_ref[...].astype(BF16), wproj_ref[...])
    out_ref[...] = y + gate * emb
    acc = ffn(xb, acc, TAIL_LEAD_CHUNKS, n_chunks - TAIL_TRAIL_CHUNKS)
    x1_next = mix(on_ref, xn_ref)
    acc = ffn(xb, acc, n_chunks - TAIL_TRAIL_CHUNKS, n_chunks)
    ypre_sc[...] = DEEPNORM_ALPHA * x1_sc[...] + acc
    x1_sc[...] = x1_next
    x1b_sc[...] = x1_next.astype(BF16)


def _layer_spec(layer, shape):
    nd = len(shape)
    return pl.BlockSpec((None,) + tuple(shape), lambda *_: (layer,) + (0,) * nd,
                        pipeline_mode=pl.Buffered(1))


def _layer_tail(layer, o, x2, p3, w_o, g1, b1, w_in, w_down, g2, b2, w_gate, w_proj):
    T = x2.shape[0]
    tm = min(TAIL_TILE, T)
    nt = T // tm
    nh, _, hw = o.shape
    nxt = lambda s: jnp.minimum(s + 1, nt - 1)
    prv = lambda s: jnp.maximum(s - 1, 0)
    vec = _layer_spec(layer, (1, D_MODEL))
    return pl.pallas_call(
        _layer_tail_kernel,
        grid=(nt + 1,),
        in_specs=[pl.BlockSpec((nh, tm, hw), lambda s: (0, 0, 0), pipeline_mode=pl.Buffered(1)),
                  pl.BlockSpec((tm, D_MODEL), lambda s: (0, 0), pipeline_mode=pl.Buffered(1)),
                  pl.BlockSpec((nh, tm, hw), lambda s: (0, nxt(s), 0)),
                  pl.BlockSpec((tm, D_MODEL), lambda s: (nxt(s), 0)),
                  pl.BlockSpec((None, tm, D_PLE), lambda s: (layer, prv(s), 0)),
                  _const_spec((D_MODEL, D_MODEL)), vec, vec,
                  _layer_spec(layer, (D_MODEL, 2 * D_FF)), _layer_spec(layer, (D_FF, D_MODEL)), vec, vec,
                  _layer_spec(layer, (D_MODEL, D_MODEL)), _layer_spec(layer, (D_PLE, D_MODEL))],
        out_specs=pl.BlockSpec((tm, D_MODEL), lambda s: (prv(s), 0)),
        out_shape=jax.ShapeDtypeStruct((T, D_MODEL), F32),
        scratch_shapes=[pltpu.VMEM((tm, D_MODEL), F32),
                        pltpu.VMEM((tm, D_MODEL), BF16),
                        pltpu.VMEM((tm, D_MODEL), F32)],
        compiler_params=_cparams(("arbitrary",)),
        name="layer_tail",
    )(o, x2, o, x2, p3, w_o.astype(BF16), g1, b1, w_in, w_down, g2, b2, w_gate, w_proj)


def _hgrn_pre_kernel(x_ref, w_ref, lbl_ref, q_out, k_out, lf_out, v_out, sg_out, *, layer):
    n = HGRN_HEADS * HGRN_DK
    logits = lbl_ref[...]
    mx = jnp.max(logits, axis=0, keepdims=True)
    e = jnp.exp(logits - mx)
    soft = e / jnp.sum(e, axis=0, keepdims=True)
    lb = jnp.zeros((1, n), F32)
    for j in range(1, layer + 1):
        lb = lb + soft[j:j + 1, :]
    log_lb = jnp.log(lb)

    xb = x_ref[...].astype(BF16)
    blocks = [slice(lo_c, lo_c + HGRN_PRE_COLS) for lo_c in range(0, n, HGRN_PRE_COLS)]

    def proj(part, cols):
        return _dot(xb, w_ref[:, part * n + cols.start:part * n + cols.stop])

    def put(out_ref, cols, val):
        for lo_c in range(cols.start, cols.stop, HGRN_DK):
            out_ref[lo_c // HGRN_DK] = val[:, lo_c - cols.start:lo_c - cols.start + HGRN_DK].astype(out_ref.dtype)

    for cols in blocks:
        f = proj(1, cols)
        lbc = lb[:, cols]
        u = jnp.exp(-jnp.abs(f))
        t = 1.0 + u
        r = 1.0 / t
        put(k_out, cols, (1.0 - lbc) * jnp.where(f >= 0, u * r, r))
        num = jnp.where(f >= 0, 1.0 + lbc * u, lbc + u)
        log_num = jnp.maximum(jnp.log(num), jnp.maximum(log_lb[:, cols], jnp.minimum(f, 0.0)))
        put(lf_out, cols, log_num - jnp.log(t))
    for cols in blocks:
        qp = proj(0, cols)
        put(q_out, cols, qp * _sigmoid(qp))
    for cols in blocks:
        gp = proj(3, cols)
        put(sg_out, cols, gp * _sigmoid(gp))
    for cols in blocks:
        put(v_out, cols, proj(2, cols))


def _hgrn_pre(x2, w_in, lb_logits, layer):
    T = x2.shape[0]
    tm = min(HGRN_PRE_TILE, T)
    n = HGRN_HEADS * HGRN_DK
    row = lambda i: (i, 0)
    outs = [jax.ShapeDtypeStruct((HGRN_HEADS, T, HGRN_DK), dt) for dt in (F32, F32, F32, BF16, F32)]
    return pl.pallas_call(
        functools.partial(_hgrn_pre_kernel, layer=layer),
        grid=(T // tm,),
        in_specs=[pl.BlockSpec((tm, D_MODEL), row), _const_spec((D_MODEL, 4 * n)),
                  _const_spec((DEPTH, n))],
        out_specs=[pl.BlockSpec((HGRN_HEADS, tm, HGRN_DK), _head_major_rows)] * 5,
        out_shape=outs,
        compiler_params=_cparams(("parallel",)),
        name="hgrn_pre",
    )(x2, w_in.astype(BF16), lb_logits)


def _split3(g):
    g1 = g.astype(BF16)
    r1 = g - g1.astype(F32)
    g2 = r1.astype(BF16)
    g3 = (r1 - g2.astype(F32)).astype(BF16)
    return g1, g2, g3


def _level_operand(h, G, g, q, k, g_sc, rowi):
    chunk = G.shape[0]

    def bc(row):
        return jnp.broadcast_to(g_sc[row:row + 1, :], (SUBLANES, LANES))

    if h >= SUBLANES:
        pieces = []
        for r0 in range(0, chunk, SUBLANES):
            ref = bc((r0 // (2 * h)) * (2 * h) + h - 1)
            rows = slice(r0, r0 + SUBLANES)
            if (r0 // h) % 2 == 1:
                pieces.append(q[rows] * jnp.exp2(G[rows] - ref))
            else:
                pieces.append(k[rows] * jnp.exp2(ref - G[rows]))
        return jnp.concatenate(pieces, axis=0)

    up = (rowi & h) != 0
    if h == 1:
        arg = jnp.where(up, g, 0.0)
    else:
        sub = lax.broadcasted_iota(jnp.int32, (SUBLANES, LANES), 0)
        pieces = []
        for r0 in range(0, chunk, SUBLANES):
            ref = bc(r0 + h - 1)
            for start in range(2 * h, SUBLANES, 2 * h):
                ref = jnp.where(sub >= start, bc(r0 + start + h - 1), ref)
            pieces.append(ref)
        d = G - jnp.concatenate(pieces, axis=0)
        arg = jnp.where(up, d, -d)
    return jnp.where(up, q, k) * jnp.exp2(arg)


def _hgrn_rec_kernel(q_ref, k_ref, lf_ref, v_ref, sg_ref, gn_ref, o_ref,
                     st_sc, g_sc, a_sc, qg_sc, kd_sc, *, chunk, n_chunks):
    @pl.when(pl.program_id(2) == 0)
    def _():
        st_sc[...] = jnp.zeros(st_sc.shape, F32)

    ti = lax.broadcasted_iota(jnp.int32, (chunk, chunk), 0)
    si = lax.broadcasted_iota(jnp.int32, (chunk, chunk), 1)
    tri = (si <= ti).astype(BF16)
    lvl = jnp.where(ti >= si, 31 - lax.clz(jnp.bitwise_xor(ti, si)), -2)
    rowi = lax.broadcasted_iota(jnp.int32, (chunk, LANES), 0)
    gn = gn_ref[...]

    def rows_of(c):
        return slice(c * chunk, (c + 1) * chunk)

    def cumulate(c):
        rows = rows_of(c)
        g1, g2, g3 = _split3(lf_ref[0, rows, :] * LOG2_E)
        g_sc[rows, :] = _dot(tri, g1) + _dot(tri, g2) + _dot(tri, g3)

    def decay_matrix(c):
        rows = rows_of(c)
        g = lf_ref[0, rows, :] * LOG2_E
        G = g_sc[rows, :]
        q = q_ref[0, rows, :]
        k = k_ref[0, rows, :]
        g_last = G[chunk - 1:chunk, :]
        qg_sc[rows, :] = (q * jnp.exp2(G)).astype(BF16)
        kd_sc[rows, :] = (k * jnp.exp2(g_last - G)).astype(BF16)
        a = jnp.where(lvl == -1, jnp.sum(q * k, axis=1, keepdims=True), 0.0)
        h = chunk // 2
        while h >= 1:
            z32 = _level_operand(h, G, g, q, k, g_sc.at[rows, :], rowi)
            z = z32.astype(BF16)
            level = h.bit_length() - 1
            if h >= SUBLANES:
                groups = range(0, chunk, SUBLANES)
                upper = [r0 for r0 in groups if (r0 // h) % 2 == 1]
                zu = jnp.concatenate([z32[r0:r0 + SUBLANES] for r0 in upper], axis=0).astype(BF16)
                al = _dot_nt(zu, z)
                pieces = []
                for r0 in groups:
                    grp = slice(r0, r0 + SUBLANES)
                    if r0 in upper:
                        u0 = upper.index(r0) * SUBLANES
                        pieces.append(jnp.where(lvl[grp] == level, al[u0:u0 + SUBLANES], a[grp]))
                    else:
                        pieces.append(a[grp])
                a = jnp.concatenate(pieces, axis=0)
            else:
                a = jnp.where(lvl == level, _dot_nt(z, z), a)
            h //= 2
        a_sc[rows, :] = a.astype(BF16)

    def advance(c, st):
        rows = rows_of(c)
        v = v_ref[0, rows, :]
        o = _dot_nt(qg_sc[rows, :], st.astype(BF16)) + _dot(a_sc[rows, :], v)
        upd = _dot(v.astype(F32).T.astype(BF16), kd_sc[rows, :])
        st = st * jnp.exp2(g_sc[(c + 1) * chunk - 1:(c + 1) * chunk, :]) + upd
        on = _rms_norm(o, gn)
        o_ref[0, rows, :] = (on * sg_ref[0, rows, :]).astype(o_ref.dtype)
        return st

    st = st_sc[...]
    for step in range(n_chunks + 2):
        if step < n_chunks:
            cumulate(step)
        if 1 <= step <= n_chunks:
            decay_matrix(step - 1)
        if step >= 2:
            st = advance(step - 2, st)
    st_sc[...] = st


def _hgrn_rec(q, k, lf, v, sg, out_norm, B, S):
    nh = HGRN_HEADS
    ts = min(HGRN_SEQ_TILE, S)
    chunk = min(HGRN_CHUNK, ts)
    r4 = lambda t: t.reshape(nh, B, S, HGRN_DK)
    blk = pl.BlockSpec((None, 1, ts, LANES), lambda b, h, i: (h, b, i, 0))
    out = pl.pallas_call(
        functools.partial(_hgrn_rec_kernel, chunk=chunk, n_chunks=ts // chunk),
        grid=(B, nh, S // ts),
        in_specs=[blk, blk, blk, blk, blk, pl.BlockSpec((1, LANES), lambda b, h, i: (0, h))],
        out_specs=blk,
        out_shape=jax.ShapeDtypeStruct((nh, B, S, HGRN_DV), BF16),
        scratch_shapes=[pltpu.VMEM((HGRN_DV, HGRN_DK), F32),
                        pltpu.VMEM((ts, LANES), F32),
                        pltpu.VMEM((ts, chunk), BF16),
                        pltpu.VMEM((ts, LANES), BF16),
                        pltpu.VMEM((ts, LANES), BF16)],
        compiler_params=_cparams(("parallel", "parallel", "arbitrary")),
        name="hgrn_rec",
    )(r4(q), r4(k), r4(lf), r4(v), r4(sg), out_norm[None, :])
    return out.reshape(nh, B * S, HGRN_DV)


def kernel(x, p, positions, mla_w_dqkv, mla_q_norm, mla_kv_norm, mla_w_uq, mla_w_ukv, mla_w_o,
           hgrn_w_in, hgrn_lb_logits, hgrn_out_norm, hgrn_w_o, ffn_w_in, ffn_w_down,
           ln_mix_g, ln_mix_b, ln_ffn_g, ln_ffn_b, ple_w_proj, ple_w_gate):
    B, S, D = x.shape
    T = B * S
    x2 = x.reshape(T, D)
    pos2 = positions.astype(F32).reshape(1, T)
    depth = p.shape[0]
    p3 = p.reshape(depth, T, D_PLE)
    vecs = [v.reshape(depth, 1, D_MODEL) for v in (ln_mix_g, ln_mix_b, ln_ffn_g, ln_ffn_b)]
    ffn_in, ffn_down, w_gate, w_proj = (w.astype(BF16) for w in (ffn_w_in, ffn_w_down, ple_w_gate, ple_w_proj))
    for i in range(depth):
        j = i // N_MIXERS
        if i % N_MIXERS == 0:
            q, k, v = _mla_pre(x2, pos2, mla_w_dqkv[j], mla_q_norm[j], mla_kv_norm[j],
                               mla_w_uq[j], mla_w_ukv[j])
            o = _flash_attn(q, k, v, B, S)
            w_o = mla_w_o[j]
        else:
            q, k, lf, v, sg = _hgrn_pre(x2, hgrn_w_in[j], hgrn_lb_logits, i)
            o = _hgrn_rec(q, k, lf, v, sg, hgrn_out_norm[j], B, S)
            w_o = hgrn_w_o[j]
        x2 = _layer_tail(i, o, x2, p3, w_o, vecs[0], vecs[1], ffn_in, ffn_down, vecs[2], vecs[3],
                         w_gate, w_proj)
    return x2.reshape(B, S, D)
```

```python
import functools

import jax
import jax.numpy as jnp
from jax import lax
from jax.experimental import pallas as pl
from jax.experimental.pallas import tpu as pltpu

F32 = jnp.float32
BF16 = jnp.bfloat16

D_MODEL = 1024
DEPTH = 2
N_MIXERS = 2
MLA_HEADS = 8
MLA_Q_LORA = 256
MLA_KV_LORA = 256
MLA_NOPE = 128
MLA_ROPE = 64
MLA_V = 128
ROPE_THETA = 10000.0
HGRN_HEADS = 8
HGRN_DK = D_MODEL // HGRN_HEADS
HGRN_DV = D_MODEL // HGRN_HEADS
D_FF = 2816
D_PLE = 256
LN_EPS = 1e-5
RMS_EPS = 1e-6
DEEPNORM_ALPHA = (2 * DEPTH) ** 0.25

LANES = 128
SUBLANES = 8
VMEM_LIMIT_BYTES = 56 * 1024 * 1024

QK_WIDTH = 2 * LANES
NEG_BIG = -1e30

TOKEN_TILE = 512
TAIL_TILE = 512
TAIL_LEAD_CHUNKS = 2
TAIL_TRAIL_CHUNKS = 2
ATTN_Q_TILE = 4096
ATTN_KV_SUB = 256
V_WIDTH = 2 * LANES
LOG2_E = 1.4426950408889634
HGRN_SEQ_TILE = 4096
HGRN_CHUNK = 128
FFN_CHUNK = 256
HGRN_PRE_COLS = 256
HGRN_PRE_TILE = 512


def _cparams(semantics):
    return pltpu.CompilerParams(dimension_semantics=semantics,
                                vmem_limit_bytes=VMEM_LIMIT_BYTES)


def _const_spec(shape):
    nd = len(shape)
    return pl.BlockSpec(shape, lambda *_: (0,) * nd, pipeline_mode=pl.Buffered(1))


def _head_major_rows(i):
    return (0, i, 0)


def _sigmoid(x):
    return 1.0 / (1.0 + jnp.exp(-x))


def _layer_norm(y, g, b):
    mu = jnp.mean(y, axis=-1, keepdims=True)
    d = y - mu
    var = jnp.mean(d * d, axis=-1, keepdims=True)
    return d * lax.rsqrt(var + LN_EPS) * g + b


def _rms_norm(t, g):
    return t * lax.rsqrt(jnp.mean(t * t, axis=-1, keepdims=True) + RMS_EPS) * g


def _dot(a, b):
    return jnp.dot(a, b, preferred_element_type=F32)


def _dot_nt(a, b):
    return lax.dot_general(a, b, (((1,), (1,)), ((), ())), preferred_element_type=F32)


def _mla_pre_kernel(x0_ref, pos0_ref, xn_ref, posn_ref, invf_ref, wd_ref, gq_ref, gkv_ref, wq_ref, wkv_ref,
                    q_out, k_out, kr_out, v_out, cq_sc, ckv_sc, kr_sc, cos_sc, sin_sc, *, scale):
    nh = MLA_HEADS

    def front(x_ref, pos_ref):
        down = _dot(x_ref[...].astype(BF16), wd_ref[...])
        cq_sc[...] = _rms_norm(down[:, :MLA_Q_LORA], gq_ref[...]).astype(BF16)
        ckv_sc[...] = _rms_norm(down[:, MLA_Q_LORA:MLA_Q_LORA + MLA_KV_LORA], gkv_ref[...]).astype(BF16)
        ang_t = invf_ref[...] * pos_ref[...]
        cos_t = jnp.cos(ang_t).T
        sin_t = jnp.sin(ang_t).T
        pad = LANES - MLA_ROPE
        cos = jnp.concatenate([cos_t, cos_t, jnp.ones((cos_t.shape[0], pad), F32)], axis=1)
        sin = jnp.concatenate([sin_t, sin_t, jnp.zeros((sin_t.shape[0], pad), F32)], axis=1)
        base = MLA_Q_LORA + MLA_KV_LORA
        kr_sc[...] = (down[:, base:base + LANES] * cos + down[:, base + LANES:base + 2 * LANES] * sin).astype(BF16)
        cos_sc[...] = cos
        sin_sc[...] = sin

    @pl.when(pl.program_id(0) == 0)
    def _():
        front(x0_ref, pos0_ref)

    cos = cos_sc[...]
    sin = sin_sc[...]
    kr = kr_sc[...]
    qall = _dot(cq_sc[...], wq_ref[...])
    kvall = _dot(ckv_sc[...], wkv_ref[...])
    front(xn_ref, posn_ref)
    for h in range(nh):
        lo = h * LANES
        qn = qall[:, lo:lo + LANES]
        qr = qall[:, nh * LANES + lo:nh * LANES + lo + LANES]
        qx = qall[:, 2 * nh * LANES + lo:2 * nh * LANES + lo + LANES]
        q_out[h, :, :LANES] = (qn * scale).astype(BF16)
        q_out[h, :, LANES:] = ((qr * cos + qx * sin) * scale).astype(BF16)
        k_out[h] = kvall[:, lo:lo + LANES].astype(BF16)
        v_out[h] = kvall[:, nh * LANES + lo:nh * LANES + lo + LANES].astype(BF16)
    kr_out[...] = kr


def _rope_slot(w):
    half = MLA_ROPE // 2
    t1, t2 = w[..., :half], w[..., half:]
    z = jnp.zeros(w.shape[:-1] + (LANES - MLA_ROPE,), w.dtype)
    return jnp.concatenate([t1, t2, z], -1), jnp.concatenate([-t2, t1, z], -1)


def _mla_pre(x2, pos2, w_dqkv, q_norm, kv_norm, w_uq, w_ukv):
    T = x2.shape[0]
    tm = min(TOKEN_TILE, T)
    nh = MLA_HEADS
    base = MLA_Q_LORA + MLA_KV_LORA
    slot, rot = _rope_slot(w_dqkv[:, base:])
    wd = jnp.concatenate([w_dqkv[:, :base], slot, rot], axis=1).astype(BF16)
    wq3 = w_uq.reshape(MLA_Q_LORA, nh, MLA_NOPE + MLA_ROPE)
    qslot, qrot = _rope_slot(wq3[:, :, MLA_NOPE:])
    wq = jnp.concatenate([wq3[:, :, :MLA_NOPE].reshape(MLA_Q_LORA, nh * LANES),
                          qslot.reshape(MLA_Q_LORA, nh * LANES),
                          qrot.reshape(MLA_Q_LORA, nh * LANES)], axis=1).astype(BF16)
    wkv3 = w_ukv.reshape(MLA_KV_LORA, nh, MLA_NOPE + MLA_V)
    wkv = jnp.concatenate([wkv3[:, :, :MLA_NOPE].reshape(MLA_KV_LORA, nh * LANES),
                           wkv3[:, :, MLA_NOPE:].reshape(MLA_KV_LORA, nh * LANES)], axis=1).astype(BF16)
    inv_freq = ROPE_THETA ** (-jnp.arange(0, MLA_ROPE, 2, dtype=F32) / MLA_ROPE)
    invf = inv_freq[:, None]
    scale = (MLA_NOPE + MLA_ROPE) ** -0.5 * LOG2_E
    nt = T // tm
    first = lambda s: (0, 0)
    nxt = lambda s: (jnp.minimum(s + 1, nt - 1), 0)
    nxt_cols = lambda s: (0, jnp.minimum(s + 1, nt - 1))
    return pl.pallas_call(
        functools.partial(_mla_pre_kernel, scale=scale),
        grid=(nt,),
        in_specs=[pl.BlockSpec((tm, D_MODEL), first, pipeline_mode=pl.Buffered(1)),
                  pl.BlockSpec((1, tm), first, pipeline_mode=pl.Buffered(1)),
                  pl.BlockSpec((tm, D_MODEL), nxt),
                  pl.BlockSpec((1, tm), nxt_cols),
                  _const_spec(invf.shape), _const_spec(wd.shape),
                  _const_spec((1, MLA_Q_LORA)), _const_spec((1, MLA_KV_LORA)),
                  _const_spec(wq.shape), _const_spec(wkv.shape)],
        out_specs=[pl.BlockSpec((nh, tm, QK_WIDTH), _head_major_rows),
                   pl.BlockSpec((nh, tm, MLA_NOPE), _head_major_rows),
                   pl.BlockSpec((tm, LANES), lambda s: (s, 0)),
                   pl.BlockSpec((nh, tm, MLA_V), _head_major_rows)],
        out_shape=[jax.ShapeDtypeStruct((nh, T, QK_WIDTH), BF16),
                   jax.ShapeDtypeStruct((nh, T, MLA_NOPE), BF16),
                   jax.ShapeDtypeStruct((T, LANES), BF16),
                   jax.ShapeDtypeStruct((nh, T, MLA_V), BF16)],
        scratch_shapes=[pltpu.VMEM((tm, MLA_Q_LORA), BF16),
                        pltpu.VMEM((tm, MLA_KV_LORA), BF16),
                        pltpu.VMEM((tm, LANES), BF16),
                        pltpu.VMEM((tm, LANES), F32),
                        pltpu.VMEM((tm, LANES), F32)],
        compiler_params=_cparams(("arbitrary",)),
        name="mla_pre",
    )(x2, pos2, x2, pos2, invf, wd, q_norm[None, :], kv_norm[None, :], wq, wkv)


def _attn_kernel(q_ref, k_ref, kr_ref, v_ref, o_ref, m_sc, acc_sc, *, tq, sub):
    qi = pl.program_id(2)
    m_sc[...] = jnp.full(m_sc.shape, NEG_BIG, F32)
    acc_sc[...] = jnp.zeros(acc_sc.shape, F32)
    nsub = tq // sub

    def sub_step(kv_start, row0, diag_col0):
        k = jnp.concatenate([k_ref[0, 0, pl.ds(kv_start, sub), :],
                             kr_ref[0, pl.ds(kv_start, sub), :]], axis=1)
        s = _dot_nt(q_ref[0, 0, row0:, :], k)
        if diag_col0 is not None:
            assert row0 == diag_col0
            r = lax.broadcasted_iota(jnp.int32, (sub, sub), 0)
            c = lax.broadcasted_iota(jnp.int32, (sub, sub), 1)
            top = jnp.where(c <= r, s[:sub], NEG_BIG)
            s = top if s.shape[0] == sub else jnp.concatenate([top, s[sub:]], axis=0)
        m_prev = m_sc[row0:, :]
        m_new = jnp.maximum(m_prev, jnp.max(s, axis=1, keepdims=True))
        alpha = jnp.exp2(m_prev - m_new)
        p = jnp.exp2(s - jnp.tile(m_new, (1, sub // LANES))).astype(BF16)
        v = jnp.concatenate([v_ref[0, 0, pl.ds(kv_start, sub), :], jnp.ones((sub, LANES), BF16)], axis=1)
        acc_sc[row0:, :] = jnp.tile(alpha, (1, 2)) * acc_sc[row0:, :] + _dot(p, v)
        m_sc[row0:, :] = m_new

    def body(j, carry):
        base = j * tq
        for c in range(nsub):
            sub_step(pl.multiple_of(base + c * sub, sub), 0, None)
        return carry

    lax.fori_loop(0, qi, body, 0)
    base = qi * tq
    for c in range(nsub):
        sub_step(pl.multiple_of(base + c * sub, sub), c * sub, c * sub)
    acc = acc_sc[...]
    o_ref[0, 0] = (acc[:, :MLA_V] / acc[:, MLA_V:]).astype(o_ref.dtype)


def _flash_attn(q, k, kr, v, B, S):
    nh = MLA_HEADS
    tq = min(ATTN_Q_TILE, S)
    sub = min(ATTN_KV_SUB, tq)
    q4 = q.reshape(nh, B, S, QK_WIDTH)
    k4 = k.reshape(nh, B, S, MLA_NOPE)
    kr3 = kr.reshape(B, S, LANES)
    v4 = v.reshape(nh, B, S, MLA_V)
    whole_seq = lambda b, h, i: (h, b, 0, 0)
    q_tile = lambda b, h, i: (h, b, i, 0)
    out = pl.pallas_call(
        functools.partial(_attn_kernel, tq=tq, sub=sub),
        grid=(B, nh, S // tq),
        in_specs=[pl.BlockSpec((1, 1, tq, QK_WIDTH), q_tile),
                  pl.BlockSpec((1, 1, S, MLA_NOPE), whole_seq),
                  pl.BlockSpec((1, S, LANES), lambda b, h, i: (b, 0, 0)),
                  pl.BlockSpec((1, 1, S, MLA_V), whole_seq)],
        out_specs=pl.BlockSpec((1, 1, tq, MLA_V), q_tile),
        out_shape=jax.ShapeDtypeStruct((nh, B, S, MLA_V), BF16),
        scratch_shapes=[pltpu.VMEM((tq, LANES), F32),
                        pltpu.VMEM((tq, V_WIDTH), F32)],
        compiler_params=_cparams(("parallel", "parallel", "arbitrary")),
        name="flash_attn",
    )(q4, k4, kr3, v4)
    return out.reshape(nh, B * S, MLA_V)


def _layer_tail_kernel(o0_ref, x0_ref, on_ref, xn_ref, p_ref, wo_ref, g1_ref, b1_ref, win_ref, wdown_ref,
                       g_ref, b_ref, wgate_ref, wproj_ref, out_ref, x1_sc, x1b_sc, ypre_sc):
    def mix(o_ref, x_ref):
        o = jnp.concatenate([o_ref[h] for h in range(o_ref.shape[0])], axis=1)
        return _layer_norm(DEEPNORM_ALPHA * x_ref[...] + _dot(o, wo_ref[...]), g1_ref[...], b1_ref[...])

    def ffn(xb, acc, first, last):
        for c in range(first, last):
            lo = c * FFN_CHUNK
            gate = _dot(xb, win_ref[:, lo:lo + FFN_CHUNK])
            up = _dot(xb, win_ref[:, D_FF + lo:D_FF + lo + FFN_CHUNK])
            act = (gate * _sigmoid(gate) * up).astype(BF16)
            acc = acc + _dot(act, wdown_ref[lo:lo + FFN_CHUNK, :])
        return acc

    @pl.when(pl.program_id(0) == 0)
    def _():
        x1 = mix(o0_ref, x0_ref)
        x1_sc[...] = x1
        x1b_sc[...] = x1.astype(BF16)
        ypre_sc[...] = jnp.zeros(ypre_sc.shape, F32)

    n_chunks = D_FF // FFN_CHUNK
    y = _layer_norm(ypre_sc[...], g_ref[...], b_ref[...])
    xb = x1b_sc[...]
    acc = ffn(xb, jnp.zeros(x1_sc.shape, F32), 0, TAIL_LEAD_CHUNKS)
    gate = _sigmoid(_dot(y.astype(BF16), wgate_ref[...]))
    emb = _dot(p_ref[...].astype(BF16), wproj_ref[...])
    out_ref[...] = y + gate * emb
    acc = ffn(xb, acc, TAIL_LEAD_CHUNKS, n_chunks - TAIL_TRAIL_CHUNKS)
    x1_next = mix(on_ref, xn_ref)
    acc = ffn(xb, acc, n_chunks - TAIL_TRAIL_CHUNKS, n_chunks)
    ypre_sc[...] = DEEPNORM_ALPHA * x1_sc[...] + acc
    x1_sc[...] = x1_next
    x1b_sc[...] = x1_next.astype(BF16)


def _layer_spec(layer, shape):
    nd = len(shape)
    return pl.BlockSpec((None,) + tuple(shape), lambda *_: (layer,) + (0,) * nd,
                        pipeline_mode=pl.Buffered(1))


def _layer_tail(layer, o, x2, p3, w_o, g1, b1, w_in, w_down, g2, b2, w_gate, w_proj):
    T = x2.shape[0]
    tm = min(TAIL_TILE, T)
    nt = T // tm
    nh, _, hw = o.shape
    nxt = lambda s: jnp.minimum(s + 1, nt - 1)
    prv = lambda s: jnp.maximum(s - 1, 0)
    vec = _layer_spec(layer, (1, D_MODEL))
    return pl.pallas_call(
        _layer_tail_kernel,
        grid=(nt + 1,),
        in_specs=[pl.BlockSpec((nh, tm, hw), lambda s: (0, 0, 0), pipeline_mode=pl.Buffered(1)),
                  pl.BlockSpec((tm, D_MODEL), lambda s: (0, 0), pipeline_mode=pl.Buffered(1)),
                  pl.BlockSpec((nh, tm, hw), lambda s: (0, nxt(s), 0)),
                  pl.BlockSpec((tm, D_MODEL), lambda s: (nxt(s), 0)),
                  pl.BlockSpec((None, tm, D_PLE), lambda s: (layer, prv(s), 0)),
                  _const_spec((D_MODEL, D_MODEL)), vec, vec,
                  _layer_spec(layer, (D_MODEL, 2 * D_FF)), _layer_spec(layer, (D_FF, D_MODEL)), vec, vec,
                  _layer_spec(layer, (D_MODEL, D_MODEL)), _layer_spec(layer, (D_PLE, D_MODEL))],
        out_specs=pl.BlockSpec((tm, D_MODEL), lambda s: (prv(s), 0)),
        out_shape=jax.ShapeDtypeStruct((T, D_MODEL), F32),
        scratch_shapes=[pltpu.VMEM((tm, D_MODEL), F32),
                        pltpu.VMEM((tm, D_MODEL), BF16),
                        pltpu.VMEM((tm, D_MODEL), F32)],
        compiler_params=_cparams(("arbitrary",)),
        name="layer_tail",
    )(o, x2, o, x2, p3, w_o.astype(BF16), g1, b1, w_in, w_down, g2, b2, w_gate, w_proj)


def _hgrn_pre_kernel(x_ref, w_ref, lbl_ref, q_out, k_out, lf_out, v_out, sg_out, *, layer):
    n = HGRN_HEADS * HGRN_DK
    logits = lbl_ref[...]
    mx = jnp.max(logits, axis=0, keepdims=True)
    e = jnp.exp(logits - mx)
    soft = e / jnp.sum(e, axis=0, keepdims=True)
    lb = jnp.zeros((1, n), F32)
    for j in range(1, layer + 1):
        lb = lb + soft[j:j + 1, :]
    log_lb = jnp.log(lb)

    xb = x_ref[...].astype(BF16)
    blocks = [slice(lo_c, lo_c + HGRN_PRE_COLS) for lo_c in range(0, n, HGRN_PRE_COLS)]

    def proj(part, cols):
        return _dot(xb, w_ref[:, part * n + cols.start:part * n + cols.stop])

    def put(out_ref, cols, val):
        for lo_c in range(cols.start, cols.stop, HGRN_DK):
            out_ref[lo_c // HGRN_DK] = val[:, lo_c - cols.start:lo_c - cols.start + HGRN_DK].astype(out_ref.dtype)

    for cols in blocks:
        f = proj(1, cols)
        lbc = lb[:, cols]
        u = jnp.exp(-jnp.abs(f))
        t = 1.0 + u
        r = 1.0 / t
        put(k_out, cols, (1.0 - lbc) * jnp.where(f >= 0, u * r, r))
        num = jnp.where(f >= 0, 1.0 + lbc * u, lbc + u)
        log_num = jnp.maximum(jnp.log(num), jnp.maximum(log_lb[:, cols], jnp.minimum(f, 0.0)))
        put(lf_out, cols, log_num - jnp.log(t))
    for cols in blocks:
        qp = proj(0, cols)
        put(q_out, cols, qp * _sigmoid(qp))
    for cols in blocks:
        gp = proj(3, cols)
        put(sg_out, cols, gp * _sigmoid(gp))
    for cols in blocks:
        put(v_out, cols, proj(2, cols))


def _hgrn_pre(x2, w_in, lb_logits, layer):
    T = x2.shape[0]
    tm = min(HGRN_PRE_TILE, T)
    n = HGRN_HEADS * HGRN_DK
    row = lambda i: (i, 0)
    outs = [jax.ShapeDtypeStruct((HGRN_HEADS, T, HGRN_DK), dt) for dt in (F32, F32, F32, BF16, F32)]
    return pl.pallas_call(
        functools.partial(_hgrn_pre_kernel, layer=layer),
        grid=(T // tm,),
        in_specs=[pl.BlockSpec((tm, D_MODEL), row), _const_spec((D_MODEL, 4 * n)),
                  _const_spec((DEPTH, n))],
        out_specs=[pl.BlockSpec((HGRN_HEADS, tm, HGRN_DK), _head_major_rows)] * 5,
        out_shape=outs,
        compiler_params=_cparams(("parallel",)),
        name="hgrn_pre",
    )(x2, w_in.astype(BF16), lb_logits)


def _split3(g):
    g1 = g.astype(BF16)
    r1 = g - g1.astype(F32)
    g2 = r1.astype(BF16)
    g3 = (r1 - g2.astype(F32)).astype(BF16)
    return g1, g2, g3


def _level_operand(h, G, g, q, k, g_sc, rowi):
    chunk = G.shape[0]

    def bc(row):
        return jnp.broadcast_to(g_sc[row:row + 1, :], (SUBLANES, LANES))

    if h >= SUBLANES:
        pieces = []
        for r0 in range(0, chunk, SUBLANES):
            ref = bc((r0 // (2 * h)) * (2 * h) + h - 1)
            rows = slice(r0, r0 + SUBLANES)
            if (r0 // h) % 2 == 1:
                pieces.append(q[rows] * jnp.exp2(G[rows] - ref))
            else:
                pieces.append(k[rows] * jnp.exp2(ref - G[rows]))
        return jnp.concatenate(pieces, axis=0)

    up = (rowi & h) != 0
    if h == 1:
        arg = jnp.where(up, g, 0.0)
    else:
        sub = lax.broadcasted_iota(jnp.int32, (SUBLANES, LANES), 0)
        pieces = []
        for r0 in range(0, chunk, SUBLANES):
            ref = bc(r0 + h - 1)
            for start in range(2 * h, SUBLANES, 2 * h):
                ref = jnp.where(sub >= start, bc(r0 + start + h - 1), ref)
            pieces.append(ref)
        d = G - jnp.concatenate(pieces, axis=0)
        arg = jnp.where(up, d, -d)
    return jnp.where(up, q, k) * jnp.exp2(arg)


def _hgrn_rec_kernel(q_ref, k_ref, lf_ref, v_ref, sg_ref, gn_ref, o_ref,
                     st_sc, g_sc, a_sc, qg_sc, kd_sc, *, chunk, n_chunks):
    @pl.when(pl.program_id(2) == 0)
    def _():
        st_sc[...] = jnp.zeros(st_sc.shape, F32)

    ti = lax.broadcasted_iota(jnp.int32, (chunk, chunk), 0)
    si = lax.broadcasted_iota(jnp.int32, (chunk, chunk), 1)
    tri = (si <= ti).astype(BF16)
    lvl = jnp.where(ti >= si, 31 - lax.clz(jnp.bitwise_xor(ti, si)), -2)
    rowi = lax.broadcasted_iota(jnp.int32, (chunk, LANES), 0)
    gn = gn_ref[...]

    def rows_of(c):
        return slice(c * chunk, (c + 1) * chunk)

    def cumulate(c):
        rows = rows_of(c)
        g1, g2, g3 = _split3(lf_ref[0, rows, :] * LOG2_E)
        g_sc[rows, :] = _dot(tri, g1) + _dot(tri, g2) + _dot(tri, g3)

    def decay_matrix(c):
        rows = rows_of(c)
        g = lf_ref[0, rows, :] * LOG2_E
        G = g_sc[rows, :]
        q = q_ref[0, rows, :]
        k = k_ref[0, rows, :]
        g_last = G[chunk - 1:chunk, :]
        qg_sc[rows, :] = (q * jnp.exp2(G)).astype(BF16)
        kd_sc[rows, :] = (k * jnp.exp2(g_last - G)).astype(BF16)
        a = jnp.where(lvl == -1, jnp.sum(q * k, axis=1, keepdims=True), 0.0)
        h = chunk // 2
        while h >= 1:
            z32 = _level_operand(h, G, g, q, k, g_sc.at[rows, :], rowi)
            z = z32.astype(BF16)
            level = h.bit_length() - 1
            if h >= SUBLANES:
                groups = range(0, chunk, SUBLANES)
                upper = [r0 for r0 in groups if (r0 // h) % 2 == 1]
                zu = jnp.concatenate([z32[r0:r0 + SUBLANES] for r0 in upper], axis=0).astype(BF16)
                al = _dot_nt(zu, z)
                pieces = []
                for r0 in groups:
                    grp = slice(r0, r0 + SUBLANES)
                    if r0 in upper:
                        u0 = upper.index(r0) * SUBLANES
                        pieces.append(jnp.where(lvl[grp] == level, al[u0:u0 + SUBLANES], a[grp]))
                    else:
                        pieces.append(a[grp])
                a = jnp.concatenate(pieces, axis=0)
            else:
                a = jnp.where(lvl == level, _dot_nt(z, z), a)
            h //= 2
        a_sc[rows, :] = a.astype(BF16)

    def advance(c, st):
        rows = rows_of(c)
        v = v_ref[0, rows, :]
        o = _dot_nt(qg_sc[rows, :], st.astype(BF16)) + _dot(a_sc[rows, :], v)
        upd = _dot(v.astype(F32).T.astype(BF16), kd_sc[rows, :])
        st = st * jnp.exp2(g_sc[(c + 1) * chunk - 1:(c + 1) * chunk, :]) + upd
        on = _rms_norm(o, gn)
        o_ref[0, rows, :] = (on * sg_ref[0, rows, :]).astype(o_ref.dtype)
        return st

    st = st_sc[...]
    for step in range(n_chunks + 2):
        if step < n_chunks:
            cumulate(step)
        if 1 <= step <= n_chunks:
            decay_matrix(step - 1)
        if step >= 2:
            st = advance(step - 2, st)
    st_sc[...] = st


def _hgrn_rec(q, k, lf, v, sg, out_norm, B, S):
    nh = HGRN_HEADS
    ts = min(HGRN_SEQ_TILE, S)
    chunk = min(HGRN_CHUNK, ts)
    r4 = lambda t: t.reshape(nh, B, S, HGRN_DK)
    blk = pl.BlockSpec((None, 1, ts, LANES), lambda b, h, i: (h, b, i, 0))
    out = pl.pallas_call(
        functools.partial(_hgrn_rec_kernel, chunk=chunk, n_chunks=ts // chunk),
        grid=(B, nh, S // ts),
        in_specs=[blk, blk, blk, blk, blk, pl.BlockSpec((1, LANES), lambda b, h, i: (0, h))],
        out_specs=blk,
        out_shape=jax.ShapeDtypeStruct((nh, B, S, HGRN_DV), BF16),
        scratch_shapes=[pltpu.VMEM((HGRN_DV, HGRN_DK), F32),
                        pltpu.VMEM((ts, LANES), F32),
                        pltpu.VMEM((ts, chunk), BF16),
                        pltpu.VMEM((ts, LANES), BF16),
                        pltpu.VMEM((ts, LANES), BF16)],
        compiler_params=_cparams(("parallel", "parallel", "arbitrary")),
        name="hgrn_rec",
    )(r4(q), r4(k), r4(lf), r4(v), r4(sg), out_norm[None, :])
    return out.reshape(nh, B * S, HGRN_DV)


def kernel(x, p, positions, mla_w_dqkv, mla_q_norm, mla_kv_norm, mla_w_uq, mla_w_ukv, mla_w_o,
           hgrn_w_in, hgrn_lb_logits, hgrn_out_norm, hgrn_w_o, ffn_w_in, ffn_w_down,
           ln_mix_g, ln_mix_b, ln_ffn_g, ln_ffn_b, ple_w_proj, ple_w_gate):
    B, S, D = x.shape
    T = B * S
    x2 = x.reshape(T, D)
    pos2 = positions.astype(F32).reshape(1, T)
    depth = p.shape[0]
    p3 = p.reshape(depth, T, D_PLE)
    vecs = [v.reshape(depth, 1, D_MODEL) for v in (ln_mix_g, ln_mix_b, ln_ffn_g, ln_ffn_b)]
    ffn_in, ffn_down, w_gate, w_proj = (w.astype(BF16) for w in (ffn_w_in, ffn_w_down, ple_w_gate, ple_w_proj))
    for i in range(depth):
        j = i // N_MIXERS
        if i % N_MIXERS == 0:
            q, k, kr, v = _mla_pre(x2, pos2, mla_w_dqkv[j], mla_q_norm[j], mla_kv_norm[j],
                                   mla_w_uq[j], mla_w_ukv[j])
            o = _flash_attn(q, k, kr, v, B, S)
            w_o = mla_w_o[j]
        else:
            q, k, lf, v, sg = _hgrn_pre(x2, hgrn_w_in[j], hgrn_lb_logits, i)
            o = _hgrn_rec(q, k, lf, v, sg, hgrn_out_norm[j], B, S)
            w_o = hgrn_w_o[j]
        x2 = _layer_tail(i, o, x2, p3, w_o, vecs[0], vecs[1], ffn_in, ffn_down, vecs[2], vecs[3],
                         w_gate, w_proj)
    return x2.reshape(B, S, D)
```

```python
import functools

import jax
import jax.numpy as jnp
from jax import lax
from jax.experimental import pallas as pl
from jax.experimental.pallas import tpu as pltpu

F32 = jnp.float32
BF16 = jnp.bfloat16

D_MODEL = 1024
DEPTH = 2
N_MIXERS = 2
MLA_HEADS = 8
MLA_Q_LORA = 256
MLA_KV_LORA = 256
MLA_NOPE = 128
MLA_ROPE = 64
MLA_V = 128
ROPE_THETA = 10000.0
HGRN_HEADS = 8
HGRN_DK = D_MODEL // HGRN_HEADS
HGRN_DV = D_MODEL // HGRN_HEADS
D_FF = 2816
D_PLE = 256
LN_EPS = 1e-5
RMS_EPS = 1e-6
DEEPNORM_ALPHA = (2 * DEPTH) ** 0.25

LANES = 128
SUBLANES = 8
VMEM_LIMIT_BYTES = 56 * 1024 * 1024

QK_WIDTH = 2 * LANES
NEG_BIG = -1e30

TOKEN_TILE = 512
TAIL_TILE = 512
TAIL_LEAD_CHUNKS = 2
TAIL_TRAIL_CHUNKS = 2
ATTN_Q_TILE = 4096
ATTN_KV_SUB = 256
V_WIDTH = 2 * LANES
LOG2_E = 1.4426950408889634
HGRN_SEQ_TILE = 4096
HGRN_CHUNK = 128
FFN_CHUNK = 256
HGRN_PRE_COLS = 256
HGRN_PRE_TILE = 512


def _cparams(semantics):
    return pltpu.CompilerParams(dimension_semantics=semantics,
                                vmem_limit_bytes=VMEM_LIMIT_BYTES)


def _const_spec(shape):
    nd = len(shape)
    return pl.BlockSpec(shape, lambda *_: (0,) * nd, pipeline_mode=pl.Buffered(1))


def _head_major_rows(i):
    return (0, i, 0)


def _sigmoid(x):
    return 1.0 / (1.0 + jnp.exp(-x))


def _layer_norm(y, g, b):
    mu = jnp.mean(y, axis=-1, keepdims=True)
    d = y - mu
    var = jnp.mean(d * d, axis=-1, keepdims=True)
    return d * lax.rsqrt(var + LN_EPS) * g + b


def _rms_norm(t, g):
    return t * lax.rsqrt(jnp.mean(t * t, axis=-1, keepdims=True) + RMS_EPS) * g


def _dot(a, b):
    return jnp.dot(a, b, preferred_element_type=F32)


def _dot_nt(a, b):
    return lax.dot_general(a, b, (((1,), (1,)), ((), ())), preferred_element_type=F32)


def _mla_pre_kernel(x0_ref, pos0_ref, xn_ref, posn_ref, invf_ref, wd_ref, gq_ref, gkv_ref, wq_ref, wkv_ref,
                    q_out, k_out, kr_out, v_out, cq_sc, ckv_sc, kr_sc, cos_sc, sin_sc, *, scale):
    nh = MLA_HEADS

    def front(x_ref, pos_ref):
        down = _dot(x_ref[...].astype(BF16), wd_ref[...])
        cq_sc[...] = _rms_norm(down[:, :MLA_Q_LORA], gq_ref[...]).astype(BF16)
        ckv_sc[...] = _rms_norm(down[:, MLA_Q_LORA:MLA_Q_LORA + MLA_KV_LORA], gkv_ref[...]).astype(BF16)
        ang_t = invf_ref[...] * pos_ref[...]
        cos_t = jnp.cos(ang_t).T
        sin_t = jnp.sin(ang_t).T
        pad = LANES - MLA_ROPE
        cos = jnp.concatenate([cos_t, cos_t, jnp.ones((cos_t.shape[0], pad), F32)], axis=1)
        sin = jnp.concatenate([sin_t, sin_t, jnp.zeros((sin_t.shape[0], pad), F32)], axis=1)
        base = MLA_Q_LORA + MLA_KV_LORA
        kr_sc[...] = (down[:, base:base + LANES] * cos + down[:, base + LANES:base + 2 * LANES] * sin).astype(BF16)
        cos_sc[...] = cos
        sin_sc[...] = sin

    @pl.when(pl.program_id(0) == 0)
    def _():
        front(x0_ref, pos0_ref)

    cos = cos_sc[...]
    sin = sin_sc[...]
    kr = kr_sc[...]
    qall = _dot(cq_sc[...], wq_ref[...])
    kvall = _dot(ckv_sc[...], wkv_ref[...])
    front(xn_ref, posn_ref)
    for h in range(nh):
        lo = h * LANES
        qn = qall[:, lo:lo + LANES]
        qr = qall[:, nh * LANES + lo:nh * LANES + lo + LANES]
        qx = qall[:, 2 * nh * LANES + lo:2 * nh * LANES + lo + LANES]
        q_out[h, :, :LANES] = (qn * scale).astype(BF16)
        q_out[h, :, LANES:] = ((qr * cos + qx * sin) * scale).astype(BF16)
        k_out[h] = kvall[:, lo:lo + LANES].astype(BF16)
        v_out[h] = kvall[:, nh * LANES + lo:nh * LANES + lo + LANES].astype(BF16)
    kr_out[...] = kr


def _rope_slot(w):
    half = MLA_ROPE // 2
    t1, t2 = w[..., :half], w[..., half:]
    z = jnp.zeros(w.shape[:-1] + (LANES - MLA_ROPE,), w.dtype)
    return jnp.concatenate([t1, t2, z], -1), jnp.concatenate([-t2, t1, z], -1)


def _mla_pre(x2, pos2, w_dqkv, q_norm, kv_norm, w_uq, w_ukv):
    T = x2.shape[0]
    tm = min(TOKEN_TILE, T)
    nh = MLA_HEADS
    base = MLA_Q_LORA + MLA_KV_LORA
    slot, rot = _rope_slot(w_dqkv[:, base:])
    wd = jnp.concatenate([w_dqkv[:, :base], slot, rot], axis=1).astype(BF16)
    wq3 = w_uq.reshape(MLA_Q_LORA, nh, MLA_NOPE + MLA_ROPE)
    qslot, qrot = _rope_slot(wq3[:, :, MLA_NOPE:])
    wq = jnp.concatenate([wq3[:, :, :MLA_NOPE].reshape(MLA_Q_LORA, nh * LANES),
                          qslot.reshape(MLA_Q_LORA, nh * LANES),
                          qrot.reshape(MLA_Q_LORA, nh * LANES)], axis=1).astype(BF16)
    wkv3 = w_ukv.reshape(MLA_KV_LORA, nh, MLA_NOPE + MLA_V)
    wkv = jnp.concatenate([wkv3[:, :, :MLA_NOPE].reshape(MLA_KV_LORA, nh * LANES),
                           wkv3[:, :, MLA_NOPE:].reshape(MLA_KV_LORA, nh * LANES)], axis=1).astype(BF16)
    inv_freq = ROPE_THETA ** (-jnp.arange(0, MLA_ROPE, 2, dtype=F32) / MLA_ROPE)
    invf = inv_freq[:, None]
    scale = (MLA_NOPE + MLA_ROPE) ** -0.5 * LOG2_E
    nt = T // tm
    first = lambda s: (0, 0)
    nxt = lambda s: (jnp.minimum(s + 1, nt - 1), 0)
    nxt_cols = lambda s: (0, jnp.minimum(s + 1, nt - 1))
    return pl.pallas_call(
        functools.partial(_mla_pre_kernel, scale=scale),
        grid=(nt,),
        in_specs=[pl.BlockSpec((tm, D_MODEL), first, pipeline_mode=pl.Buffered(1)),
                  pl.BlockSpec((1, tm), first, pipeline_mode=pl.Buffered(1)),
                  pl.BlockSpec((tm, D_MODEL), nxt),
                  pl.BlockSpec((1, tm), nxt_cols),
                  _const_spec(invf.shape), _const_spec(wd.shape),
                  _const_spec((1, MLA_Q_LORA)), _const_spec((1, MLA_KV_LORA)),
                  _const_spec(wq.shape), _const_spec(wkv.shape)],
        out_specs=[pl.BlockSpec((nh, tm, QK_WIDTH), _head_major_rows),
                   pl.BlockSpec((nh, tm, MLA_NOPE), _head_major_rows),
                   pl.BlockSpec((tm, LANES), lambda s: (s, 0)),
                   pl.BlockSpec((nh, tm, MLA_V), _head_major_rows)],
        out_shape=[jax.ShapeDtypeStruct((nh, T, QK_WIDTH), BF16),
                   jax.ShapeDtypeStruct((nh, T, MLA_NOPE), BF16),
                   jax.ShapeDtypeStruct((T, LANES), BF16),
                   jax.ShapeDtypeStruct((nh, T, MLA_V), BF16)],
        scratch_shapes=[pltpu.VMEM((tm, MLA_Q_LORA), BF16),
                        pltpu.VMEM((tm, MLA_KV_LORA), BF16),
                        pltpu.VMEM((tm, LANES), BF16),
                        pltpu.VMEM((tm, LANES), F32),
                        pltpu.VMEM((tm, LANES), F32)],
        compiler_params=_cparams(("arbitrary",)),
        name="mla_pre",
    )(x2, pos2, x2, pos2, invf, wd, q_norm[None, :], kv_norm[None, :], wq, wkv)


def _attn_kernel(q_ref, k_ref, kr_ref, v_ref, o_ref, m_sc, acc_sc, *, tq, sub):
    qi = pl.program_id(2)
    m_sc[...] = jnp.full(m_sc.shape, NEG_BIG, F32)
    acc_sc[...] = jnp.zeros(acc_sc.shape, F32)
    nsub = tq // sub

    def sub_step(kv_start, row0, diag_col0):
        k = jnp.concatenate([k_ref[0, 0, pl.ds(kv_start, sub), :],
                             kr_ref[0, pl.ds(kv_start, sub), :]], axis=1)
        s = _dot_nt(q_ref[0, 0, row0:, :], k)
        if diag_col0 is not None:
            assert row0 == diag_col0
            r = lax.broadcasted_iota(jnp.int32, (sub, sub), 0)
            c = lax.broadcasted_iota(jnp.int32, (sub, sub), 1)
            top = jnp.where(c <= r, s[:sub], NEG_BIG)
            s = top if s.shape[0] == sub else jnp.concatenate([top, s[sub:]], axis=0)
        m_prev = m_sc[row0:, :]
        m_new = jnp.maximum(m_prev, jnp.max(s, axis=1, keepdims=True))
        alpha = jnp.exp2(m_prev - m_new)
        p = jnp.exp2(s - jnp.tile(m_new, (1, sub // LANES))).astype(BF16)
        v = jnp.concatenate([v_ref[0, 0, pl.ds(kv_start, sub), :], jnp.ones((sub, LANES), BF16)], axis=1)
        acc_sc[row0:, :] = jnp.tile(alpha, (1, 2)) * acc_sc[row0:, :] + _dot(p, v)
        m_sc[row0:, :] = m_new

    def body(j, carry):
        base = j * tq
        for c in range(nsub):
            sub_step(pl.multiple_of(base + c * sub, sub), 0, None)
        return carry

    lax.fori_loop(0, qi, body, 0)
    base = qi * tq
    for c in range(nsub):
        sub_step(pl.multiple_of(base + c * sub, sub), c * sub, c * sub)
    acc = acc_sc[...]
    o_ref[0, 0] = (acc[:, :MLA_V] / acc[:, MLA_V:]).astype(o_ref.dtype)


def _flash_attn(q, k, kr, v, B, S):
    nh = MLA_HEADS
    tq = min(ATTN_Q_TILE, S)
    sub = min(ATTN_KV_SUB, tq)
    q4 = q.reshape(nh, B, S, QK_WIDTH)
    k4 = k.reshape(nh, B, S, MLA_NOPE)
    kr3 = kr.reshape(B, S, LANES)
    v4 = v.reshape(nh, B, S, MLA_V)
    whole_seq = lambda b, h, i: (h, b, 0, 0)
    q_tile = lambda b, h, i: (h, b, i, 0)
    out = pl.pallas_call(
        functools.partial(_attn_kernel, tq=tq, sub=sub),
        grid=(B, nh, S // tq),
        in_specs=[pl.BlockSpec((1, 1, tq, QK_WIDTH), q_tile),
                  pl.BlockSpec((1, 1, S, MLA_NOPE), whole_seq),
                  pl.BlockSpec((1, S, LANES), lambda b, h, i: (b, 0, 0)),
                  pl.BlockSpec((1, 1, S, MLA_V), whole_seq)],
        out_specs=pl.BlockSpec((1, 1, tq, MLA_V), q_tile),
        out_shape=jax.ShapeDtypeStruct((nh, B, S, MLA_V), BF16),
        scratch_shapes=[pltpu.VMEM((tq, LANES), F32),
                        pltpu.VMEM((tq, V_WIDTH), F32)],
        compiler_params=_cparams(("parallel", "parallel", "arbitrary")),
        name="flash_attn",
    )(q4, k4, kr3, v4)
    return out.reshape(nh, B * S, MLA_V)


def _layer_tail_kernel(o0_ref, x0_ref, on_ref, xn_ref, p_ref, wo_ref, g1_ref, b1_ref, win_ref, wdown_ref,
                       g_ref, b_ref, wgate_ref, wproj_ref, out_ref, x1_sc, x1b_sc, ypre_sc):
    def mix(o_ref, x_ref):
        o = jnp.concatenate([o_ref[h] for h in range(o_ref.shape[0])], axis=1)
        return _layer_norm(DEEPNORM_ALPHA * x_ref[...] + _dot(o, wo_ref[...]), g1_ref[...], b1_ref[...])

    def ffn(xb, acc, first, last):
        for c in range(first, last):
            lo = c * FFN_CHUNK
            gate = _dot(xb, win_ref[:, lo:lo + FFN_CHUNK])
            up = _dot(xb, win_ref[:, D_FF + lo:D_FF + lo + FFN_CHUNK])
            act = (gate * _sigmoid(gate) * up).astype(BF16)
            acc = acc + _dot(act, wdown_ref[lo:lo + FFN_CHUNK, :])
        return acc

    @pl.when(pl.program_id(0) == 0)
    def _():
        x1 = mix(o0_ref, x0_ref)
        x1_sc[...] = x1
        x1b_sc[...] = x1.astype(BF16)
        ypre_sc[...] = jnp.zeros(ypre_sc.shape, F32)

    n_chunks = D_FF // FFN_CHUNK
    y = _layer_norm(ypre_sc[...], g_ref[...], b_ref[...])
    xb = x1b_sc[...]
    acc = ffn(xb, jnp.zeros(x1_sc.shape, F32), 0, TAIL_LEAD_CHUNKS)
    gate = _sigmoid(_dot(y.astype(BF16), wgate_ref[...]))
    emb = _dot(p_ref[...].astype(BF16), wproj_ref[...])
    out_ref[...] = y + gate * emb
    acc = ffn(xb, acc, TAIL_LEAD_CHUNKS, n_chunks - TAIL_TRAIL_CHUNKS)
    x1_next = mix(on_ref, xn_ref)
    acc = ffn(xb, acc, n_chunks - TAIL_TRAIL_CHUNKS, n_chunks)
    ypre_sc[...] = DEEPNORM_ALPHA * x1_sc[...] + acc
    x1_sc[...] = x1_next
    x1b_sc[...] = x1_next.astype(BF16)


def _layer_spec(layer, shape):
    nd = len(shape)
    return pl.BlockSpec((None,) + tuple(shape), lambda *_: (layer,) + (0,) * nd,
                        pipeline_mode=pl.Buffered(1))


def _layer_tail(layer, o, x2, p3, w_o, g1, b1, w_in, w_down, g2, b2, w_gate, w_proj):
    T = x2.shape[0]
    tm = min(TAIL_TILE, T)
    nt = T // tm
    nh, _, hw = o.shape
    nxt = lambda s: jnp.minimum(s + 1, nt - 1)
    prv = lambda s: jnp.maximum(s - 1, 0)
    vec = _layer_spec(layer, (1, D_MODEL))
    return pl.pallas_call(
        _layer_tail_kernel,
        grid=(nt + 1,),
        in_specs=[pl.BlockSpec((nh, tm, hw), lambda s: (0, 0, 0), pipeline_mode=pl.Buffered(1)),
                  pl.BlockSpec((tm, D_MODEL), lambda s: (0, 0), pipeline_mode=pl.Buffered(1)),
                  pl.BlockSpec((nh, tm, hw), lambda s: (0, nxt(s), 0)),
                  pl.BlockSpec((tm, D_MODEL), lambda s: (nxt(s), 0)),
                  pl.BlockSpec((None, tm, D_PLE), lambda s: (layer, prv(s), 0)),
                  _const_spec((D_MODEL, D_MODEL)), vec, vec,
                  _layer_spec(layer, (D_MODEL, 2 * D_FF)), _layer_spec(layer, (D_FF, D_MODEL)), vec, vec,
                  _layer_spec(layer, (D_MODEL, D_MODEL)), _layer_spec(layer, (D_PLE, D_MODEL))],
        out_specs=pl.BlockSpec((tm, D_MODEL), lambda s: (prv(s), 0)),
        out_shape=jax.ShapeDtypeStruct((T, D_MODEL), F32),
        scratch_shapes=[pltpu.VMEM((tm, D_MODEL), F32),
                        pltpu.VMEM((tm, D_MODEL), BF16),
                        pltpu.VMEM((tm, D_MODEL), F32)],
        compiler_params=_cparams(("arbitrary",)),
        name="layer_tail",
    )(o, x2, o, x2, p3, w_o.astype(BF16), g1, b1, w_in, w_down, g2, b2, w_gate, w_proj)


def _hgrn_pre_kernel(x_ref, w_ref, lbl_ref, q_out, k_out, lf_out, v_out, sg_out, *, layer):
    n = HGRN_HEADS * HGRN_DK
    logits = lbl_ref[...]
    mx = jnp.max(logits, axis=0, keepdims=True)
    e = jnp.exp(logits - mx)
    soft = e / jnp.sum(e, axis=0, keepdims=True)
    lb = jnp.zeros((1, n), F32)
    for j in range(1, layer + 1):
        lb = lb + soft[j:j + 1, :]
    log_lb = jnp.log(lb)

    xb = x_ref[...].astype(BF16)
    blocks = [slice(lo_c, lo_c + HGRN_PRE_COLS) for lo_c in range(0, n, HGRN_PRE_COLS)]

    def proj(part, cols):
        return _dot(xb, w_ref[:, part * n + cols.start:part * n + cols.stop])

    def put(out_ref, cols, val):
        for lo_c in range(cols.start, cols.stop, HGRN_DK):
            out_ref[lo_c // HGRN_DK] = val[:, lo_c - cols.start:lo_c - cols.start + HGRN_DK].astype(out_ref.dtype)

    for cols in blocks:
        qp = proj(0, cols)
        put(q_out, cols, qp * _sigmoid(qp))
        f = proj(1, cols)
        lbc = lb[:, cols]
        u = jnp.exp(-jnp.abs(f))
        t = 1.0 + u
        r = 1.0 / t
        put(k_out, cols, (1.0 - lbc) * jnp.where(f >= 0, u * r, r))
        num = jnp.where(f >= 0, 1.0 + lbc * u, lbc + u)
        log_num = jnp.maximum(jnp.log(num), jnp.maximum(log_lb[:, cols], jnp.minimum(f, 0.0)))
        put(lf_out, cols, log_num - jnp.log(t))
        put(v_out, cols, proj(2, cols))
        gp = proj(3, cols)
        put(sg_out, cols, gp * _sigmoid(gp))


def _hgrn_pre(x2, w_in, lb_logits, layer):
    T = x2.shape[0]
    tm = min(HGRN_PRE_TILE, T)
    n = HGRN_HEADS * HGRN_DK
    row = lambda i: (i, 0)
    outs = [jax.ShapeDtypeStruct((HGRN_HEADS, T, HGRN_DK), dt) for dt in (F32, F32, F32, BF16, F32)]
    return pl.pallas_call(
        functools.partial(_hgrn_pre_kernel, layer=layer),
        grid=(T // tm,),
        in_specs=[pl.BlockSpec((tm, D_MODEL), row), _const_spec((D_MODEL, 4 * n)),
                  _const_spec((DEPTH, n))],
        out_specs=[pl.BlockSpec((HGRN_HEADS, tm, HGRN_DK), _head_major_rows)] * 5,
        out_shape=outs,
        compiler_params=_cparams(("parallel",)),
        name="hgrn_pre",
    )(x2, w_in.astype(BF16), lb_logits)


def _split3(g):
    g1 = g.astype(BF16)
    r1 = g - g1.astype(F32)
    g2 = r1.astype(BF16)
    g3 = (r1 - g2.astype(F32)).astype(BF16)
    return g1, g2, g3


def _level_operand(h, G, g, q, k, g_sc, rowi):
    chunk = G.shape[0]

    def bc(row):
        return jnp.broadcast_to(g_sc[row:row + 1, :], (SUBLANES, LANES))

    if h >= SUBLANES:
        pieces = []
        for r0 in range(0, chunk, SUBLANES):
            ref = bc((r0 // (2 * h)) * (2 * h) + h - 1)
            rows = slice(r0, r0 + SUBLANES)
            if (r0 // h) % 2 == 1:
                pieces.append(q[rows] * jnp.exp2(G[rows] - ref))
            else:
                pieces.append(k[rows] * jnp.exp2(ref - G[rows]))
        return jnp.concatenate(pieces, axis=0)

    up = (rowi & h) != 0
    if h == 1:
        arg = jnp.where(up, g, 0.0)
    else:
        sub = lax.broadcasted_iota(jnp.int32, (SUBLANES, LANES), 0)
        pieces = []
        for r0 in range(0, chunk, SUBLANES):
            ref = bc(r0 + h - 1)
            for start in range(2 * h, SUBLANES, 2 * h):
                ref = jnp.where(sub >= start, bc(r0 + start + h - 1), ref)
            pieces.append(ref)
        d = G - jnp.concatenate(pieces, axis=0)
        arg = jnp.where(up, d, -d)
    return jnp.where(up, q, k) * jnp.exp2(arg)


def _hgrn_rec_kernel(q_ref, k_ref, lf_ref, v_ref, sg_ref, gn_ref, o_ref,
                     st_sc, g_sc, a_sc, qg_sc, kd_sc, *, chunk, n_chunks):
    @pl.when(pl.program_id(2) == 0)
    def _():
        st_sc[...] = jnp.zeros(st_sc.shape, F32)

    ti = lax.broadcasted_iota(jnp.int32, (chunk, chunk), 0)
    si = lax.broadcasted_iota(jnp.int32, (chunk, chunk), 1)
    tri = (si <= ti).astype(BF16)
    lvl = jnp.where(ti >= si, 31 - lax.clz(jnp.bitwise_xor(ti, si)), -2)
    rowi = lax.broadcasted_iota(jnp.int32, (chunk, LANES), 0)
    gn = gn_ref[...]

    def rows_of(c):
        return slice(c * chunk, (c + 1) * chunk)

    def cumulate(c):
        rows = rows_of(c)
        g1, g2, g3 = _split3(lf_ref[0, rows, :] * LOG2_E)
        g_sc[rows, :] = _dot(tri, g1) + _dot(tri, g2) + _dot(tri, g3)

    def decay_matrix(c):
        rows = rows_of(c)
        g = lf_ref[0, rows, :] * LOG2_E
        G = g_sc[rows, :]
        q = q_ref[0, rows, :]
        k = k_ref[0, rows, :]
        g_last = G[chunk - 1:chunk, :]
        qg_sc[rows, :] = (q * jnp.exp2(G)).astype(BF16)
        kd_sc[rows, :] = (k * jnp.exp2(g_last - G)).astype(BF16)
        a = jnp.where(lvl == -1, jnp.sum(q * k, axis=1, keepdims=True), 0.0)
        h = chunk // 2
        while h >= 1:
            z32 = _level_operand(h, G, g, q, k, g_sc.at[rows, :], rowi)
            z = z32.astype(BF16)
            level = h.bit_length() - 1
            if h >= SUBLANES:
                groups = range(0, chunk, SUBLANES)
                upper = [r0 for r0 in groups if (r0 // h) % 2 == 1]
                zu = jnp.concatenate([z32[r0:r0 + SUBLANES] for r0 in upper], axis=0).astype(BF16)
                al = _dot_nt(zu, z)
                pieces = []
                for r0 in groups:
                    grp = slice(r0, r0 + SUBLANES)
                    if r0 in upper:
                        u0 = upper.index(r0) * SUBLANES
                        pieces.append(jnp.where(lvl[grp] == level, al[u0:u0 + SUBLANES], a[grp]))
                    else:
                        pieces.append(a[grp])
                a = jnp.concatenate(pieces, axis=0)
            else:
                a = jnp.where(lvl == level, _dot_nt(z, z), a)
            h //= 2
        a_sc[rows, :] = a.astype(BF16)

    def advance(c, st):
        rows = rows_of(c)
        v = v_ref[0, rows, :]
        o = _dot_nt(qg_sc[rows, :], st.astype(BF16)) + _dot(a_sc[rows, :], v)
        upd = _dot(v.astype(F32).T.astype(BF16), kd_sc[rows, :])
        st = st * jnp.exp2(g_sc[(c + 1) * chunk - 1:(c + 1) * chunk, :]) + upd
        on = _rms_norm(o, gn)
        o_ref[0, rows, :] = (on * sg_ref[0, rows, :]).astype(o_ref.dtype)
        return st

    st = st_sc[...]
    for step in range(n_chunks + 2):
        if step < n_chunks:
            cumulate(step)
        if 1 <= step <= n_chunks:
            decay_matrix(step - 1)
        if step >= 2:
            st = advance(step - 2, st)
    st_sc[...] = st


def _hgrn_rec(q, k, lf, v, sg, out_norm, B, S):
    nh = HGRN_HEADS
    ts = min(HGRN_SEQ_TILE, S)
    chunk = min(HGRN_CHUNK, ts)
    r4 = lambda t: t.reshape(nh, B, S, HGRN_DK)
    blk = pl.BlockSpec((None, 1, ts, LANES), lambda b, h, i: (h, b, i, 0))
    out = pl.pallas_call(
        functools.partial(_hgrn_rec_kernel, chunk=chunk, n_chunks=ts // chunk),
        grid=(B, nh, S // ts),
        in_specs=[blk, blk, blk, blk, blk, pl.BlockSpec((1, LANES), lambda b, h, i: (0, h))],
        out_specs=blk,
        out_shape=jax.ShapeDtypeStruct((nh, B, S, HGRN_DV), BF16),
        scratch_shapes=[pltpu.VMEM((HGRN_DV, HGRN_DK), F32),
                        pltpu.VMEM((ts, LANES), F32),
                        pltpu.VMEM((ts, chunk), BF16),
                        pltpu.VMEM((ts, LANES), BF16),
                        pltpu.VMEM((ts, LANES), BF16)],
        compiler_params=_cparams(("parallel", "parallel", "arbitrary")),
        name="hgrn_rec",
    )(r4(q), r4(k), r4(lf), r4(v), r4(sg), out_norm[None, :])
    return out.reshape(nh, B * S, HGRN_DV)


def kernel(x, p, positions, mla_w_dqkv, mla_q_norm, mla_kv_norm, mla_w_uq, mla_w_ukv, mla_w_o,
           hgrn_w_in, hgrn_lb_logits, hgrn_out_norm, hgrn_w_o, ffn_w_in, ffn_w_down,
           ln_mix_g, ln_mix_b, ln_ffn_g, ln_ffn_b, ple_w_proj, ple_w_gate):
    B, S, D = x.shape
    T = B * S
    x2 = x.reshape(T, D)
    pos2 = positions.astype(F32).reshape(1, T)
    depth = p.shape[0]
    p3 = p.reshape(depth, T, D_PLE)
    vecs = [v.reshape(depth, 1, D_MODEL) for v in (ln_mix_g, ln_mix_b, ln_ffn_g, ln_ffn_b)]
    ffn_in, ffn_down, w_gate, w_proj = (w.astype(BF16) for w in (ffn_w_in, ffn_w_down, ple_w_gate, ple_w_proj))
    for i in range(depth):
        j = i // N_MIXERS
        if i % N_MIXERS == 0:
            q, k, kr, v = _mla_pre(x2, pos2, mla_w_dqkv[j], mla_q_norm[j], mla_kv_norm[j],
                                   mla_w_uq[j], mla_w_ukv[j])
            o = _flash_attn(q, k, kr, v, B, S)
            w_o = mla_w_o[j]
        else:
            q, k, lf, v, sg = _hgrn_pre(x2, hgrn_w_in[j], hgrn_lb_logits, i)
            o = _hgrn_rec(q, k, lf, v, sg, hgrn_out_norm[j], B, S)
            w_o = hgrn_w_o[j]
        x2 = _layer_tail(i, o, x2, p3, w_o, vecs[0], vecs[1], ffn_in, ffn_down, vecs[2], vecs[3],
                         w_gate, w_proj)
    return x2.reshape(B, S, D)
```

```python
import functools

import jax
import jax.numpy as jnp
from jax import lax
from jax.experimental import pallas as pl
from jax.experimental.pallas import tpu as pltpu

F32 = jnp.float32
BF16 = jnp.bfloat16

D_MODEL = 1024
DEPTH = 2
N_MIXERS = 2
MLA_HEADS = 8
MLA_Q_LORA = 256
MLA_KV_LORA = 256
MLA_NOPE = 128
MLA_ROPE = 64
MLA_V = 128
ROPE_THETA = 10000.0
HGRN_HEADS = 8
HGRN_DK = D_MODEL // HGRN_HEADS
HGRN_DV = D_MODEL // HGRN_HEADS
D_FF = 2816
D_PLE = 256
LN_EPS = 1e-5
RMS_EPS = 1e-6
DEEPNORM_ALPHA = (2 * DEPTH) ** 0.25

LANES = 128
SUBLANES = 8
VMEM_LIMIT_BYTES = 56 * 1024 * 1024

QK_WIDTH = 2 * LANES
NEG_BIG = -1e30

TOKEN_TILE = 512
TAIL_TILE = 512
TAIL_LEAD_CHUNKS = 2
TAIL_TRAIL_CHUNKS = 2
ATTN_Q_TILE = 4096
ATTN_KV_SUB = 256
V_WIDTH = 2 * LANES
LOG2_E = 1.4426950408889634
HGRN_SEQ_TILE = 4096
HGRN_CHUNK = 128
FFN_CHUNK = 256
HGRN_PRE_COLS = 256
HGRN_PRE_TILE = 512


def _cparams(semantics):
    return pltpu.CompilerParams(dimension_semantics=semantics,
                                vmem_limit_bytes=VMEM_LIMIT_BYTES)


def _const_spec(shape):
    nd = len(shape)
    return pl.BlockSpec(shape, lambda *_: (0,) * nd, pipeline_mode=pl.Buffered(1))


def _head_major_rows(i):
    return (0, i, 0)


def _sigmoid(x):
    return 1.0 / (1.0 + jnp.exp(-x))


def _layer_norm(y, g, b):
    mu = jnp.mean(y, axis=-1, keepdims=True)
    d = y - mu
    var = jnp.mean(d * d, axis=-1, keepdims=True)
    return d * lax.rsqrt(var + LN_EPS) * g + b


def _rms_norm(t, g):
    return t * lax.rsqrt(jnp.mean(t * t, axis=-1, keepdims=True) + RMS_EPS) * g


def _dot(a, b):
    return jnp.dot(a, b, preferred_element_type=F32)


def _dot_nt(a, b):
    return lax.dot_general(a, b, (((1,), (1,)), ((), ())), preferred_element_type=F32)


def _mla_pre_kernel(x0_ref, pos0_ref, xn_ref, posn_ref, invf_ref, wd_ref, gq_ref, gkv_ref, wq_ref, wkv_ref,
                    q_out, k_out, kr_out, v_out, cq_sc, ckv_sc, kr_sc, cos_sc, sin_sc, *, scale):
    nh = MLA_HEADS

    def front(x_ref, pos_ref):
        down = _dot(x_ref[...].astype(BF16), wd_ref[...])
        cq_sc[...] = _rms_norm(down[:, :MLA_Q_LORA], gq_ref[...]).astype(BF16)
        ckv_sc[...] = _rms_norm(down[:, MLA_Q_LORA:MLA_Q_LORA + MLA_KV_LORA], gkv_ref[...]).astype(BF16)
        ang_t = invf_ref[...] * pos_ref[...]
        cos_t = jnp.cos(ang_t).T
        sin_t = jnp.sin(ang_t).T
        pad = LANES - MLA_ROPE
        cos = jnp.concatenate([cos_t, cos_t, jnp.ones((cos_t.shape[0], pad), F32)], axis=1)
        sin = jnp.concatenate([sin_t, sin_t, jnp.zeros((sin_t.shape[0], pad), F32)], axis=1)
        base = MLA_Q_LORA + MLA_KV_LORA
        kr_sc[...] = (down[:, base:base + LANES] * cos + down[:, base + LANES:base + 2 * LANES] * sin).astype(BF16)
        cos_sc[...] = cos
        sin_sc[...] = sin

    @pl.when(pl.program_id(0) == 0)
    def _():
        front(x0_ref, pos0_ref)

    cos = cos_sc[...]
    sin = sin_sc[...]
    kr = kr_sc[...]
    qall = _dot(cq_sc[...], wq_ref[...])
    kvall = _dot(ckv_sc[...], wkv_ref[...])
    front(xn_ref, posn_ref)
    for h in range(nh):
        lo = h * LANES
        qn = qall[:, lo:lo + LANES]
        qr = qall[:, nh * LANES + lo:nh * LANES + lo + LANES]
        qx = qall[:, 2 * nh * LANES + lo:2 * nh * LANES + lo + LANES]
        q_out[h, :, :LANES] = (qn * scale).astype(BF16)
        q_out[h, :, LANES:] = ((qr * cos + qx * sin) * scale).astype(BF16)
        k_out[h] = kvall[:, lo:lo + LANES].astype(BF16)
        v_out[h] = kvall[:, nh * LANES + lo:nh * LANES + lo + LANES].astype(BF16)
    kr_out[...] = kr


def _rope_slot(w):
    half = MLA_ROPE // 2
    t1, t2 = w[..., :half], w[..., half:]
    z = jnp.zeros(w.shape[:-1] + (LANES - MLA_ROPE,), w.dtype)
    return jnp.concatenate([t1, t2, z], -1), jnp.concatenate([-t2, t1, z], -1)


def _mla_pre(x2, pos2, w_dqkv, q_norm, kv_norm, w_uq, w_ukv):
    T = x2.shape[0]
    tm = min(TOKEN_TILE, T)
    nh = MLA_HEADS
    base = MLA_Q_LORA + MLA_KV_LORA
    slot, rot = _rope_slot(w_dqkv[:, base:])
    wd = jnp.concatenate([w_dqkv[:, :base], slot, rot], axis=1).astype(BF16)
    wq3 = w_uq.reshape(MLA_Q_LORA, nh, MLA_NOPE + MLA_ROPE)
    qslot, qrot = _rope_slot(wq3[:, :, MLA_NOPE:])
    wq = jnp.concatenate([wq3[:, :, :MLA_NOPE].reshape(MLA_Q_LORA, nh * LANES),
                          qslot.reshape(MLA_Q_LORA, nh * LANES),
                          qrot.reshape(MLA_Q_LORA, nh * LANES)], axis=1).astype(BF16)
    wkv3 = w_ukv.reshape(MLA_KV_LORA, nh, MLA_NOPE + MLA_V)
    wkv = jnp.concatenate([wkv3[:, :, :MLA_NOPE].reshape(MLA_KV_LORA, nh * LANES),
                           wkv3[:, :, MLA_NOPE:].reshape(MLA_KV_LORA, nh * LANES)], axis=1).astype(BF16)
    inv_freq = ROPE_THETA ** (-jnp.arange(0, MLA_ROPE, 2, dtype=F32) / MLA_ROPE)
    invf = inv_freq[:, None]
    scale = (MLA_NOPE + MLA_ROPE) ** -0.5 * LOG2_E
    nt = T // tm
    first = lambda s: (0, 0)
    nxt = lambda s: (jnp.minimum(s + 1, nt - 1), 0)
    nxt_cols = lambda s: (0, jnp.minimum(s + 1, nt - 1))
    return pl.pallas_call(
        functools.partial(_mla_pre_kernel, scale=scale),
        grid=(nt,),
        in_specs=[pl.BlockSpec((tm, D_MODEL), first, pipeline_mode=pl.Buffered(1)),
                  pl.BlockSpec((1, tm), first, pipeline_mode=pl.Buffered(1)),
                  pl.BlockSpec((tm, D_MODEL), nxt),
                  pl.BlockSpec((1, tm), nxt_cols),
                  _const_spec(invf.shape), _const_spec(wd.shape),
                  _const_spec((1, MLA_Q_LORA)), _const_spec((1, MLA_KV_LORA)),
                  _const_spec(wq.shape), _const_spec(wkv.shape)],
        out_specs=[pl.BlockSpec((nh, tm, QK_WIDTH), _head_major_rows),
                   pl.BlockSpec((nh, tm, MLA_NOPE), _head_major_rows),
                   pl.BlockSpec((tm, LANES), lambda s: (s, 0)),
                   pl.BlockSpec((nh, tm, MLA_V), _head_major_rows)],
        out_shape=[jax.ShapeDtypeStruct((nh, T, QK_WIDTH), BF16),
                   jax.ShapeDtypeStruct((nh, T, MLA_NOPE), BF16),
                   jax.ShapeDtypeStruct((T, LANES), BF16),
                   jax.ShapeDtypeStruct((nh, T, MLA_V), BF16)],
        scratch_shapes=[pltpu.VMEM((tm, MLA_Q_LORA), BF16),
                        pltpu.VMEM((tm, MLA_KV_LORA), BF16),
                        pltpu.VMEM((tm, LANES), BF16),
                        pltpu.VMEM((tm, LANES), F32),
                        pltpu.VMEM((tm, LANES), F32)],
        compiler_params=_cparams(("arbitrary",)),
        name="mla_pre",
    )(x2, pos2, x2, pos2, invf, wd, q_norm[None, :], kv_norm[None, :], wq, wkv)


def _attn_kernel(q_ref, k_ref, kr_ref, v_ref, o_ref, m_sc, acc_sc, *, tq, sub):
    qi = pl.program_id(2)
    m_sc[...] = jnp.full(m_sc.shape, NEG_BIG, F32)
    acc_sc[...] = jnp.zeros(acc_sc.shape, F32)
    nsub = tq // sub

    def sub_step(kv_start, row0, diag_col0):
        k = jnp.concatenate([k_ref[0, 0, pl.ds(kv_start, sub), :],
                             kr_ref[0, pl.ds(kv_start, sub), :]], axis=1)
        s = _dot_nt(q_ref[0, 0, row0:, :], k)
        if diag_col0 is not None:
            assert row0 == diag_col0
            r = lax.broadcasted_iota(jnp.int32, (sub, sub), 0)
            c = lax.broadcasted_iota(jnp.int32, (sub, sub), 1)
            top = jnp.where(c <= r, s[:sub], NEG_BIG)
            s = top if s.shape[0] == sub else jnp.concatenate([top, s[sub:]], axis=0)
        m_prev = m_sc[row0:, :]
        m_new = jnp.maximum(m_prev, jnp.max(s, axis=1, keepdims=True))
        alpha = jnp.exp2(m_prev - m_new)
        p = jnp.exp2(s - jnp.tile(m_new, (1, sub // LANES))).astype(BF16)
        v = jnp.concatenate([v_ref[0, 0, pl.ds(kv_start, sub), :], jnp.ones((sub, LANES), BF16)], axis=1)
        acc_sc[row0:, :] = jnp.tile(alpha, (1, 2)) * acc_sc[row0:, :] + _dot(p, v)
        m_sc[row0:, :] = m_new

    def body(j, carry):
        base = j * tq
        for c in range(nsub):
            sub_step(pl.multiple_of(base + c * sub, sub), 0, None)
        return carry

    lax.fori_loop(0, qi, body, 0)
    base = qi * tq
    for c in range(nsub):
        sub_step(pl.multiple_of(base + c * sub, sub), c * sub, c * sub)
    acc = acc_sc[...]
    o_ref[0, 0] = (acc[:, :MLA_V] / acc[:, MLA_V:]).astype(o_ref.dtype)


def _flash_attn(q, k, kr, v, B, S):
    nh = MLA_HEADS
    tq = min(ATTN_Q_TILE, S)
    sub = min(ATTN_KV_SUB, tq)
    q4 = q.reshape(nh, B, S, QK_WIDTH)
    k4 = k.reshape(nh, B, S, MLA_NOPE)
    kr3 = kr.reshape(B, S, LANES)
    v4 = v.reshape(nh, B, S, MLA_V)
    whole_seq = lambda b, h, i: (h, b, 0, 0)
    q_tile = lambda b, h, i: (h, b, i, 0)
    out = pl.pallas_call(
        functools.partial(_attn_kernel, tq=tq, sub=sub),
        grid=(B, nh, S // tq),
        in_specs=[pl.BlockSpec((1, 1, tq, QK_WIDTH), q_tile),
                  pl.BlockSpec((1, 1, S, MLA_NOPE), whole_seq),
                  pl.BlockSpec((1, S, LANES), lambda b, h, i: (b, 0, 0)),
                  pl.BlockSpec((1, 1, S, MLA_V), whole_seq)],
        out_specs=pl.BlockSpec((1, 1, tq, MLA_V), q_tile),
        out_shape=jax.ShapeDtypeStruct((nh, B, S, MLA_V), BF16),
        scratch_shapes=[pltpu.VMEM((tq, LANES), F32),
                        pltpu.VMEM((tq, V_WIDTH), F32)],
        compiler_params=_cparams(("parallel", "parallel", "arbitrary")),
        name="flash_attn",
    )(q4, k4, kr3, v4)
    return out.reshape(nh, B * S, MLA_V)


def _layer_tail_kernel(o0_ref, x0_ref, on_ref, xn_ref, p_ref, wo_ref, g1_ref, b1_ref, win_ref, wdown_ref,
                       g_ref, b_ref, wgate_ref, wproj_ref, out_ref, x1_sc, x1b_sc, ypre_sc):
    def mix(o_ref, x_ref):
        o = jnp.concatenate([o_ref[h] for h in range(o_ref.shape[0])], axis=1)
        return _layer_norm(DEEPNORM_ALPHA * x_ref[...] + _dot(o, wo_ref[...]), g1_ref[...], b1_ref[...])

    def ffn(xb, acc, first, last):
        for c in range(first, last):
            lo = c * FFN_CHUNK
            gate = _dot(xb, win_ref[:, lo:lo + FFN_CHUNK])
            up = _dot(xb, win_ref[:, D_FF + lo:D_FF + lo + FFN_CHUNK])
            act = (gate * _sigmoid(gate) * up).astype(BF16)
            acc = acc + _dot(act, wdown_ref[lo:lo + FFN_CHUNK, :])
        return acc

    @pl.when(pl.program_id(0) == 0)
    def _():
        x1 = mix(o0_ref, x0_ref)
        x1_sc[...] = x1
        x1b_sc[...] = x1.astype(BF16)
        ypre_sc[...] = jnp.zeros(ypre_sc.shape, F32)

    n_chunks = D_FF // FFN_CHUNK
    y = _layer_norm(ypre_sc[...], g_ref[...], b_ref[...])
    xb = x1b_sc[...]
    acc = ffn(xb, jnp.zeros(x1_sc.shape, F32), 0, TAIL_LEAD_CHUNKS)
    gate = _sigmoid(_dot(y.astype(BF16), wgate_ref[...]))
    emb = _dot(p_ref[...].astype(BF16), wproj_ref[...])
    out_ref[...] = y + gate * emb
    acc = ffn(xb, acc, TAIL_LEAD_CHUNKS, n_chunks - TAIL_TRAIL_CHUNKS)
    x1_next = mix(on_ref, xn_ref)
    acc = ffn(xb, acc, n_chunks - TAIL_TRAIL_CHUNKS, n_chunks)
    ypre_sc[...] = DEEPNORM_ALPHA * x1_sc[...] + acc
    x1_sc[...] = x1_next
    x1b_sc[...] = x1_next.astype(BF16)


def _layer_spec(layer, shape):
    nd = len(shape)
    return pl.BlockSpec((None,) + tuple(shape), lambda *_: (layer,) + (0,) * nd,
                        pipeline_mode=pl.Buffered(1))


def _layer_tail(layer, o, x2, p3, w_o, g1, b1, w_in, w_down, g2, b2, w_gate, w_proj):
    T = x2.shape[0]
    tm = min(TAIL_TILE, T)
    nt = T // tm
    nh, _, hw = o.shape
    nxt = lambda s: jnp.minimum(s + 1, nt - 1)
    prv = lambda s: jnp.maximum(s - 1, 0)
    vec = _layer_spec(layer, (1, D_MODEL))
    return pl.pallas_call(
        _layer_tail_kernel,
        grid=(nt + 1,),
        in_specs=[pl.BlockSpec((nh, tm, hw), lambda s: (0, 0, 0), pipeline_mode=pl.Buffered(1)),
                  pl.BlockSpec((tm, D_MODEL), lambda s: (0, 0), pipeline_mode=pl.Buffered(1)),
                  pl.BlockSpec((nh, tm, hw), lambda s: (0, nxt(s), 0)),
                  pl.BlockSpec((tm, D_MODEL), lambda s: (nxt(s), 0)),
                  pl.BlockSpec((None, tm, D_PLE), lambda s: (layer, prv(s), 0)),
                  _const_spec((D_MODEL, D_MODEL)), vec, vec,
                  _layer_spec(layer, (D_MODEL, 2 * D_FF)), _layer_spec(layer, (D_FF, D_MODEL)), vec, vec,
                  _layer_spec(layer, (D_MODEL, D_MODEL)), _layer_spec(layer, (D_PLE, D_MODEL))],
        out_specs=pl.BlockSpec((tm, D_MODEL), lambda s: (prv(s), 0)),
        out_shape=jax.ShapeDtypeStruct((T, D_MODEL), F32),
        scratch_shapes=[pltpu.VMEM((tm, D_MODEL), F32),
                        pltpu.VMEM((tm, D_MODEL), BF16),
                        pltpu.VMEM((tm, D_MODEL), F32)],
        compiler_params=_cparams(("arbitrary",)),
        name="layer_tail",
    )(o, x2, o, x2, p3, w_o.astype(BF16), g1, b1, w_in, w_down, g2, b2, w_gate, w_proj)


def _hgrn_pre_kernel(x_ref, w_ref, lbl_ref, q_out, k_out, lf_out, v_out, sg_out, *, layer):
    n = HGRN_HEADS * HGRN_DK
    logits = lbl_ref[...]
    mx = jnp.max(logits, axis=0, keepdims=True)
    e = jnp.exp(logits - mx)
    soft = e / jnp.sum(e, axis=0, keepdims=True)
    lb = jnp.zeros((1, n), F32)
    for j in range(1, layer + 1):
        lb = lb + soft[j:j + 1, :]
    log_lb = jnp.log(lb)

    xb = x_ref[...].astype(BF16)
    blocks = [slice(lo_c, lo_c + HGRN_PRE_COLS) for lo_c in range(0, n, HGRN_PRE_COLS)]

    def proj(part, cols):
        return _dot(xb, w_ref[:, part * n + cols.start:part * n + cols.stop])

    def put(out_ref, cols, val):
        for lo_c in range(cols.start, cols.stop, HGRN_DK):
            out_ref[lo_c // HGRN_DK] = val[:, lo_c - cols.start:lo_c - cols.start + HGRN_DK].astype(out_ref.dtype)

    for cols in blocks:
        f, vp, qp, gp = (proj(part, cols) for part in (1, 2, 0, 3))
        lbc = lb[:, cols]
        u = jnp.exp(-jnp.abs(f))
        t = 1.0 + u
        r = 1.0 / t
        put(k_out, cols, (1.0 - lbc) * jnp.where(f >= 0, u * r, r))
        num = jnp.where(f >= 0, 1.0 + lbc * u, lbc + u)
        log_num = jnp.maximum(jnp.log(num), jnp.maximum(log_lb[:, cols], jnp.minimum(f, 0.0)))
        put(lf_out, cols, log_num - jnp.log(t))
        put(v_out, cols, vp)
        put(q_out, cols, qp * _sigmoid(qp))
        put(sg_out, cols, gp * _sigmoid(gp))


def _hgrn_pre(x2, w_in, lb_logits, layer):
    T = x2.shape[0]
    tm = min(HGRN_PRE_TILE, T)
    n = HGRN_HEADS * HGRN_DK
    row = lambda i: (i, 0)
    outs = [jax.ShapeDtypeStruct((HGRN_HEADS, T, HGRN_DK), dt) for dt in (F32, F32, F32, BF16, F32)]
    return pl.pallas_call(
        functools.partial(_hgrn_pre_kernel, layer=layer),
        grid=(T // tm,),
        in_specs=[pl.BlockSpec((tm, D_MODEL), row), _const_spec((D_MODEL, 4 * n)),
                  _const_spec((DEPTH, n))],
        out_specs=[pl.BlockSpec((HGRN_HEADS, tm, HGRN_DK), _head_major_rows)] * 5,
        out_shape=outs,
        compiler_params=_cparams(("parallel",)),
        name="hgrn_pre",
    )(x2, w_in.astype(BF16), lb_logits)


def _split3(g):
    g1 = g.astype(BF16)
    r1 = g - g1.astype(F32)
    g2 = r1.astype(BF16)
    g3 = (r1 - g2.astype(F32)).astype(BF16)
    return g1, g2, g3


def _level_operand(h, G, g, q, k, g_sc, rowi):
    chunk = G.shape[0]

    def bc(row):
        return jnp.broadcast_to(g_sc[row:row + 1, :], (SUBLANES, LANES))

    if h >= SUBLANES:
        pieces = []
        for r0 in range(0, chunk, SUBLANES):
            ref = bc((r0 // (2 * h)) * (2 * h) + h - 1)
            rows = slice(r0, r0 + SUBLANES)
            if (r0 // h) % 2 == 1:
                pieces.append(q[rows] * jnp.exp2(G[rows] - ref))
            else:
                pieces.append(k[rows] * jnp.exp2(ref - G[rows]))
        return jnp.concatenate(pieces, axis=0)

    up = (rowi & h) != 0
    if h == 1:
        arg = jnp.where(up, g, 0.0)
    else:
        sub = lax.broadcasted_iota(jnp.int32, (SUBLANES, LANES), 0)
        pieces = []
        for r0 in range(0, chunk, SUBLANES):
            ref = bc(r0 + h - 1)
            for start in range(2 * h, SUBLANES, 2 * h):
                ref = jnp.where(sub >= start, bc(r0 + start + h - 1), ref)
            pieces.append(ref)
        d = G - jnp.concatenate(pieces, axis=0)
        arg = jnp.where(up, d, -d)
    return jnp.where(up, q, k) * jnp.exp2(arg)


def _hgrn_rec_kernel(q_ref, k_ref, lf_ref, v_ref, sg_ref, gn_ref, o_ref,
                     st_sc, g_sc, a_sc, qg_sc, kd_sc, *, chunk, n_chunks):
    @pl.when(pl.program_id(2) == 0)
    def _():
        st_sc[...] = jnp.zeros(st_sc.shape, F32)

    ti = lax.broadcasted_iota(jnp.int32, (chunk, chunk), 0)
    si = lax.broadcasted_iota(jnp.int32, (chunk, chunk), 1)
    tri = (si <= ti).astype(BF16)
    lvl = jnp.where(ti >= si, 31 - lax.clz(jnp.bitwise_xor(ti, si)), -2)
    rowi = lax.broadcasted_iota(jnp.int32, (chunk, LANES), 0)
    gn = gn_ref[...]

    def rows_of(c):
        return slice(c * chunk, (c + 1) * chunk)

    def cumulate(c):
        rows = rows_of(c)
        g1, g2, g3 = _split3(lf_ref[0, rows, :] * LOG2_E)
        g_sc[rows, :] = _dot(tri, g1) + _dot(tri, g2) + _dot(tri, g3)

    def decay_matrix(c):
        rows = rows_of(c)
        g = lf_ref[0, rows, :] * LOG2_E
        G = g_sc[rows, :]
        q = q_ref[0, rows, :]
        k = k_ref[0, rows, :]
        g_last = G[chunk - 1:chunk, :]
        qg_sc[rows, :] = (q * jnp.exp2(G)).astype(BF16)
        kd_sc[rows, :] = (k * jnp.exp2(g_last - G)).astype(BF16)
        a = jnp.where(lvl == -1, jnp.sum(q * k, axis=1, keepdims=True), 0.0)
        h = chunk // 2
        while h >= 1:
            z32 = _level_operand(h, G, g, q, k, g_sc.at[rows, :], rowi)
            z = z32.astype(BF16)
            level = h.bit_length() - 1
            if h >= SUBLANES:
                groups = range(0, chunk, SUBLANES)
                upper = [r0 for r0 in groups if (r0 // h) % 2 == 1]
                zu = jnp.concatenate([z32[r0:r0 + SUBLANES] for r0 in upper], axis=0).astype(BF16)
                al = _dot_nt(zu, z)
                pieces = []
                for r0 in groups:
                    grp = slice(r0, r0 + SUBLANES)
                    if r0 in upper:
                        u0 = upper.index(r0) * SUBLANES
                        pieces.append(jnp.where(lvl[grp] == level, al[u0:u0 + SUBLANES], a[grp]))
                    else:
                        pieces.append(a[grp])
                a = jnp.concatenate(pieces, axis=0)
            else:
                a = jnp.where(lvl == level, _dot_nt(z, z), a)
            h //= 2
        a_sc[rows, :] = a.astype(BF16)

    def advance(c, st):
        rows = rows_of(c)
        v = v_ref[0, rows, :]
        o = _dot_nt(qg_sc[rows, :], st.astype(BF16)) + _dot(a_sc[rows, :], v)
        upd = _dot(v.astype(F32).T.astype(BF16), kd_sc[rows, :])
        st = st * jnp.exp2(g_sc[(c + 1) * chunk - 1:(c + 1) * chunk, :]) + upd
        on = _rms_norm(o, gn)
        o_ref[0, rows, :] = (on * sg_ref[0, rows, :]).astype(o_ref.dtype)
        return st

    st = st_sc[...]
    for step in range(n_chunks + 3):
        if step < n_chunks:
            cumulate(step)
        if 2 <= step <= n_chunks + 1:
            decay_matrix(step - 2)
        if step >= 3:
            st = advance(step - 3, st)
    st_sc[...] = st


def _hgrn_rec(q, k, lf, v, sg, out_norm, B, S):
    nh = HGRN_HEADS
    ts = min(HGRN_SEQ_TILE, S)
    chunk = min(HGRN_CHUNK, ts)
    r4 = lambda t: t.reshape(nh, B, S, HGRN_DK)
    blk = pl.BlockSpec((None, 1, ts, LANES), lambda b, h, i: (h, b, i, 0))
    out = pl.pallas_call(
        functools.partial(_hgrn_rec_kernel, chunk=chunk, n_chunks=ts // chunk),
        grid=(B, nh, S // ts),
        in_specs=[blk, blk, blk, blk, blk, pl.BlockSpec((1, LANES), lambda b, h, i: (0, h))],
        out_specs=blk,
        out_shape=jax.ShapeDtypeStruct((nh, B, S, HGRN_DV), BF16),
        scratch_shapes=[pltpu.VMEM((HGRN_DV, HGRN_DK), F32),
                        pltpu.VMEM((ts, LANES), F32),
                        pltpu.VMEM((ts, chunk), BF16),
                        pltpu.VMEM((ts, LANES), BF16),
                        pltpu.VMEM((ts, LANES), BF16)],
        compiler_params=_cparams(("parallel", "parallel", "arbitrary")),
        name="hgrn_rec",
    )(r4(q), r4(k), r4(lf), r4(v), r4(sg), out_norm[None, :])
    return out.reshape(nh, B * S, HGRN_DV)


def kernel(x, p, positions, mla_w_dqkv, mla_q_norm, mla_kv_norm, mla_w_uq, mla_w_ukv, mla_w_o,
           hgrn_w_in, hgrn_lb_logits, hgrn_out_norm, hgrn_w_o, ffn_w_in, ffn_w_down,
           ln_mix_g, ln_mix_b, ln_ffn_g, ln_ffn_b, ple_w_proj, ple_w_gate):
    B, S, D = x.shape
    T = B * S
    x2 = x.reshape(T, D)
    pos2 = positions.astype(F32).reshape(1, T)
    depth = p.shape[0]
    p3 = p.reshape(depth, T, D_PLE)
    vecs = [v.reshape(depth, 1, D_MODEL) for v in (ln_mix_g, ln_mix_b, ln_ffn_g, ln_ffn_b)]
    ffn_in, ffn_down, w_gate, w_proj = (w.astype(BF16) for w in (ffn_w_in, ffn_w_down, ple_w_gate, ple_w_proj))
    for i in range(depth):
        j = i // N_MIXERS
        if i % N_MIXERS == 0:
            q, k, kr, v = _mla_pre(x2, pos2, mla_w_dqkv[j], mla_q_norm[j], mla_kv_norm[j],
                                   mla_w_uq[j], mla_w_ukv[j])
            o = _flash_attn(q, k, kr, v, B, S)
            w_o = mla_w_o[j]
        else:
            q, k, lf, v, sg = _hgrn_pre(x2, hgrn_w_in[j], hgrn_lb_logits, i)
            o = _hgrn_rec(q, k, lf, v, sg, hgrn_out_norm[j], B, S)
            w_o = hgrn_w_o[j]
        x2 = _layer_tail(i, o, x2, p3, w_o, vecs[0], vecs[1], ffn_in, ffn_down, vecs[2], vecs[3],
                         w_gate, w_proj)
    return x2.reshape(B, S, D)
```

```python
import functools

import jax
import jax.numpy as jnp
from jax import lax
from jax.experimental import pallas as pl
from jax.experimental.pallas import tpu as pltpu

F32 = jnp.float32
BF16 = jnp.bfloat16

D_MODEL = 1024
DEPTH = 2
N_MIXERS = 2
MLA_HEADS = 8
MLA_Q_LORA = 256
MLA_KV_LORA = 256
MLA_NOPE = 128
MLA_ROPE = 64
MLA_V = 128
ROPE_THETA = 10000.0
HGRN_HEADS = 8
HGRN_DK = D_MODEL // HGRN_HEADS
HGRN_DV = D_MODEL // HGRN_HEADS
D_FF = 2816
D_PLE = 256
LN_EPS = 1e-5
RMS_EPS = 1e-6
DEEPNORM_ALPHA = (2 * DEPTH) ** 0.25

LANES = 128
SUBLANES = 8
VMEM_LIMIT_BYTES = 56 * 1024 * 1024

QK_WIDTH = 2 * LANES
NEG_BIG = -1e30

TOKEN_TILE = 512
TAIL_TILE = 512
TAIL_LEAD_CHUNKS = 2
TAIL_TRAIL_CHUNKS = 2
ATTN_Q_TILE = 4096
ATTN_KV_SUB = 256
V_WIDTH = 2 * LANES
LOG2_E = 1.4426950408889634
HGRN_SEQ_TILE = 4096
HGRN_CHUNK = 128
FFN_CHUNK = 256
HGRN_PRE_COLS = 256
HGRN_PRE_TILE = 512


def _cparams(semantics):
    return pltpu.CompilerParams(dimension_semantics=semantics,
                                vmem_limit_bytes=VMEM_LIMIT_BYTES)


def _const_spec(shape):
    nd = len(shape)
    return pl.BlockSpec(shape, lambda *_: (0,) * nd, pipeline_mode=pl.Buffered(1))


def _head_major_rows(i):
    return (0, i, 0)


def _sigmoid(x):
    return 1.0 / (1.0 + jnp.exp(-x))


def _layer_norm(y, g, b):
    mu = jnp.mean(y, axis=-1, keepdims=True)
    d = y - mu
    var = jnp.mean(d * d, axis=-1, keepdims=True)
    return d * lax.rsqrt(var + LN_EPS) * g + b


def _rms_norm(t, g):
    return t * lax.rsqrt(jnp.mean(t * t, axis=-1, keepdims=True) + RMS_EPS) * g


def _dot(a, b):
    return jnp.dot(a, b, preferred_element_type=F32)


def _dot_nt(a, b):
    return lax.dot_general(a, b, (((1,), (1,)), ((), ())), preferred_element_type=F32)


def _mla_pre_kernel(x0_ref, pos0_ref, xn_ref, posn_ref, invf_ref, wd_ref, gq_ref, gkv_ref, wq_ref, wkv_ref,
                    q_out, k_out, kr_out, v_out, cq_sc, ckv_sc, kr_sc, cos_sc, sin_sc, *, scale):
    nh = MLA_HEADS

    def front(x_ref, pos_ref):
        down = _dot(x_ref[...].astype(BF16), wd_ref[...])
        cq_sc[...] = _rms_norm(down[:, :MLA_Q_LORA], gq_ref[...]).astype(BF16)
        ckv_sc[...] = _rms_norm(down[:, MLA_Q_LORA:MLA_Q_LORA + MLA_KV_LORA], gkv_ref[...]).astype(BF16)
        ang_t = invf_ref[...] * pos_ref[...]
        cos_t = jnp.cos(ang_t).T
        sin_t = jnp.sin(ang_t).T
        pad = LANES - MLA_ROPE
        cos = jnp.concatenate([cos_t, cos_t, jnp.ones((cos_t.shape[0], pad), F32)], axis=1)
        sin = jnp.concatenate([sin_t, sin_t, jnp.zeros((sin_t.shape[0], pad), F32)], axis=1)
        base = MLA_Q_LORA + MLA_KV_LORA
        kr_sc[...] = (down[:, base:base + LANES] * cos + down[:, base + LANES:base + 2 * LANES] * sin).astype(BF16)
        cos_sc[...] = cos
        sin_sc[...] = sin

    @pl.when(pl.program_id(0) == 0)
    def _():
        front(x0_ref, pos0_ref)

    cos = cos_sc[...]
    sin = sin_sc[...]
    kr = kr_sc[...]
    qall = _dot(cq_sc[...], wq_ref[...])
    kvall = _dot(ckv_sc[...], wkv_ref[...])
    front(xn_ref, posn_ref)
    for h in range(nh):
        lo = h * LANES
        qn = qall[:, lo:lo + LANES]
        qr = qall[:, nh * LANES + lo:nh * LANES + lo + LANES]
        qx = qall[:, 2 * nh * LANES + lo:2 * nh * LANES + lo + LANES]
        q_out[h, :, :LANES] = (qn * scale).astype(BF16)
        q_out[h, :, LANES:] = ((qr * cos + qx * sin) * scale).astype(BF16)
        k_out[h] = kvall[:, lo:lo + LANES].astype(BF16)
        v_out[h] = kvall[:, nh * LANES + lo:nh * LANES + lo + LANES].astype(BF16)
    kr_out[...] = kr


def _rope_slot(w):
    half = MLA_ROPE // 2
    t1, t2 = w[..., :half], w[..., half:]
    z = jnp.zeros(w.shape[:-1] + (LANES - MLA_ROPE,), w.dtype)
    return jnp.concatenate([t1, t2, z], -1), jnp.concatenate([-t2, t1, z], -1)


def _mla_pre(x2, pos2, w_dqkv, q_norm, kv_norm, w_uq, w_ukv):
    T = x2.shape[0]
    tm = min(TOKEN_TILE, T)
    nh = MLA_HEADS
    base = MLA_Q_LORA + MLA_KV_LORA
    slot, rot = _rope_slot(w_dqkv[:, base:])
    wd = jnp.concatenate([w_dqkv[:, :base], slot, rot], axis=1).astype(BF16)
    wq3 = w_uq.reshape(MLA_Q_LORA, nh, MLA_NOPE + MLA_ROPE)
    qslot, qrot = _rope_slot(wq3[:, :, MLA_NOPE:])
    wq = jnp.concatenate([wq3[:, :, :MLA_NOPE].reshape(MLA_Q_LORA, nh * LANES),
                          qslot.reshape(MLA_Q_LORA, nh * LANES),
                          qrot.reshape(MLA_Q_LORA, nh * LANES)], axis=1).astype(BF16)
    wkv3 = w_ukv.reshape(MLA_KV_LORA, nh, MLA_NOPE + MLA_V)
    wkv = jnp.concatenate([wkv3[:, :, :MLA_NOPE].reshape(MLA_KV_LORA, nh * LANES),
                           wkv3[:, :, MLA_NOPE:].reshape(MLA_KV_LORA, nh * LANES)], axis=1).astype(BF16)
    inv_freq = ROPE_THETA ** (-jnp.arange(0, MLA_ROPE, 2, dtype=F32) / MLA_ROPE)
    invf = inv_freq[:, None]
    scale = (MLA_NOPE + MLA_ROPE) ** -0.5 * LOG2_E
    nt = T // tm
    first = lambda s: (0, 0)
    nxt = lambda s: (jnp.minimum(s + 1, nt - 1), 0)
    nxt_cols = lambda s: (0, jnp.minimum(s + 1, nt - 1))
    return pl.pallas_call(
        functools.partial(_mla_pre_kernel, scale=scale),
        grid=(nt,),
        in_specs=[pl.BlockSpec((tm, D_MODEL), first, pipeline_mode=pl.Buffered(1)),
                  pl.BlockSpec((1, tm), first, pipeline_mode=pl.Buffered(1)),
                  pl.BlockSpec((tm, D_MODEL), nxt),
                  pl.BlockSpec((1, tm), nxt_cols),
                  _const_spec(invf.shape), _const_spec(wd.shape),
                  _const_spec((1, MLA_Q_LORA)), _const_spec((1, MLA_KV_LORA)),
                  _const_spec(wq.shape), _const_spec(wkv.shape)],
        out_specs=[pl.BlockSpec((nh, tm, QK_WIDTH), _head_major_rows),
                   pl.BlockSpec((nh, tm, MLA_NOPE), _head_major_rows),
                   pl.BlockSpec((tm, LANES), lambda s: (s, 0)),
                   pl.BlockSpec((nh, tm, MLA_V), _head_major_rows)],
        out_shape=[jax.ShapeDtypeStruct((nh, T, QK_WIDTH), BF16),
                   jax.ShapeDtypeStruct((nh, T, MLA_NOPE), BF16),
                   jax.ShapeDtypeStruct((T, LANES), BF16),
                   jax.ShapeDtypeStruct((nh, T, MLA_V), BF16)],
        scratch_shapes=[pltpu.VMEM((tm, MLA_Q_LORA), BF16),
                        pltpu.VMEM((tm, MLA_KV_LORA), BF16),
                        pltpu.VMEM((tm, LANES), BF16),
                        pltpu.VMEM((tm, LANES), F32),
                        pltpu.VMEM((tm, LANES), F32)],
        compiler_params=_cparams(("arbitrary",)),
        name="mla_pre",
    )(x2, pos2, x2, pos2, invf, wd, q_norm[None, :], kv_norm[None, :], wq, wkv)


def _attn_kernel(q_ref, k_ref, kr_ref, v_ref, o_ref, m_sc, acc_sc, *, tq, sub):
    qi = pl.program_id(2)
    m_sc[...] = jnp.full(m_sc.shape, NEG_BIG, F32)
    acc_sc[...] = jnp.zeros(acc_sc.shape, F32)
    nsub = tq // sub

    def sub_step(kv_start, row0, diag_col0):
        k = jnp.concatenate([k_ref[0, 0, pl.ds(kv_start, sub), :],
                             kr_ref[0, pl.ds(kv_start, sub), :]], axis=1)
        s = _dot_nt(q_ref[0, 0, row0:, :], k)
        if diag_col0 is not None:
            assert row0 == diag_col0
            r = lax.broadcasted_iota(jnp.int32, (sub, sub), 0)
            c = lax.broadcasted_iota(jnp.int32, (sub, sub), 1)
            top = jnp.where(c <= r, s[:sub], NEG_BIG)
            s = top if s.shape[0] == sub else jnp.concatenate([top, s[sub:]], axis=0)
        m_prev = m_sc[row0:, :]
        m_new = jnp.maximum(m_prev, jnp.max(s, axis=1, keepdims=True))
        alpha = jnp.exp2(m_prev - m_new)
        p = jnp.exp2(s - jnp.tile(m_new, (1, sub // LANES))).astype(BF16)
        v = jnp.concatenate([v_ref[0, 0, pl.ds(kv_start, sub), :], jnp.ones((sub, LANES), BF16)], axis=1)
        acc_sc[row0:, :] = jnp.tile(alpha, (1, 2)) * acc_sc[row0:, :] + _dot(p, v)
        m_sc[row0:, :] = m_new

    def body(j, carry):
        base = j * tq
        for c in range(nsub):
            sub_step(pl.multiple_of(base + c * sub, sub), 0, None)
        return carry

    lax.fori_loop(0, qi, body, 0)
    base = qi * tq
    for c in range(nsub):
        sub_step(pl.multiple_of(base + c * sub, sub), c * sub, c * sub)
    acc = acc_sc[...]
    o_ref[0, 0] = (acc[:, :MLA_V] / acc[:, MLA_V:]).astype(o_ref.dtype)


def _flash_attn(q, k, kr, v, B, S):
    nh = MLA_HEADS
    tq = min(ATTN_Q_TILE, S)
    sub = min(ATTN_KV_SUB, tq)
    q4 = q.reshape(nh, B, S, QK_WIDTH)
    k4 = k.reshape(nh, B, S, MLA_NOPE)
    kr3 = kr.reshape(B, S, LANES)
    v4 = v.reshape(nh, B, S, MLA_V)
    whole_seq = lambda b, h, i: (h, b, 0, 0)
    q_tile = lambda b, h, i: (h, b, i, 0)
    out = pl.pallas_call(
        functools.partial(_attn_kernel, tq=tq, sub=sub),
        grid=(B, nh, S // tq),
        in_specs=[pl.BlockSpec((1, 1, tq, QK_WIDTH), q_tile),
                  pl.BlockSpec((1, 1, S, MLA_NOPE), whole_seq),
                  pl.BlockSpec((1, S, LANES), lambda b, h, i: (b, 0, 0)),
                  pl.BlockSpec((1, 1, S, MLA_V), whole_seq)],
        out_specs=pl.BlockSpec((1, 1, tq, MLA_V), q_tile),
        out_shape=jax.ShapeDtypeStruct((nh, B, S, MLA_V), BF16),
        scratch_shapes=[pltpu.VMEM((tq, LANES), F32),
                        pltpu.VMEM((tq, V_WIDTH), F32)],
        compiler_params=_cparams(("parallel", "parallel", "arbitrary")),
        name="flash_attn",
    )(q4, k4, kr3, v4)
    return out.reshape(nh, B * S, MLA_V)


def _layer_tail_kernel(o0_ref, x0_ref, on_ref, xn_ref, p_ref, wo_ref, g1_ref, b1_ref, win_ref, wdown_ref,
                       g_ref, b_ref, wgate_ref, wproj_ref, out_ref, x1_sc, x1b_sc, ypre_sc):
    def mix(o_ref, x_ref):
        o = jnp.concatenate([o_ref[h] for h in range(o_ref.shape[0])], axis=1)
        return _layer_norm(DEEPNORM_ALPHA * x_ref[...] + _dot(o, wo_ref[...]), g1_ref[...], b1_ref[...])

    def ffn(xb, acc, first, last):
        for c in range(first, last):
            lo = c * FFN_CHUNK
            gate = _dot(xb, win_ref[:, lo:lo + FFN_CHUNK])
            up = _dot(xb, win_ref[:, D_FF + lo:D_FF + lo + FFN_CHUNK])
            act = (gate * _sigmoid(gate) * up).astype(BF16)
            acc = acc + _dot(act, wdown_ref[lo:lo + FFN_CHUNK, :])
        return acc

    @pl.when(pl.program_id(0) == 0)
    def _():
        x1 = mix(o0_ref, x0_ref)
        x1_sc[...] = x1
        x1b_sc[...] = x1.astype(BF16)
        ypre_sc[...] = jnp.zeros(ypre_sc.shape, F32)

    n_chunks = D_FF // FFN_CHUNK
    y = _layer_norm(ypre_sc[...], g_ref[...], b_ref[...])
    xb = x1b_sc[...]
    acc = ffn(xb, jnp.zeros(x1_sc.shape, F32), 0, TAIL_LEAD_CHUNKS)
    gate = _sigmoid(_dot(y.astype(BF16), wgate_ref[...]))
    emb = _dot(p_ref[...].astype(BF16), wproj_ref[...])
    out_ref[...] = y + gate * emb
    acc = ffn(xb, acc, TAIL_LEAD_CHUNKS, n_chunks - TAIL_TRAIL_CHUNKS)
    x1_next = mix(on_ref, xn_ref)
    acc = ffn(xb, acc, n_chunks - TAIL_TRAIL_CHUNKS, n_chunks)
    ypre_sc[...] = DEEPNORM_ALPHA * x1_sc[...] + acc
    x1_sc[...] = x1_next
    x1b_sc[...] = x1_next.astype(BF16)


def _layer_spec(layer, shape):
    nd = len(shape)
    return pl.BlockSpec((None,) + tuple(shape), lambda *_: (layer,) + (0,) * nd,
                        pipeline_mode=pl.Buffered(1))


def _layer_tail(layer, o, x2, p3, w_o, g1, b1, w_in, w_down, g2, b2, w_gate, w_proj):
    T = x2.shape[0]
    tm = min(TAIL_TILE, T)
    nt = T // tm
    nh, _, hw = o.shape
    nxt = lambda s: jnp.minimum(s + 1, nt - 1)
    prv = lambda s: jnp.maximum(s - 1, 0)
    vec = _layer_spec(layer, (1, D_MODEL))
    return pl.pallas_call(
        _layer_tail_kernel,
        grid=(nt + 1,),
        in_specs=[pl.BlockSpec((nh, tm, hw), lambda s: (0, 0, 0), pipeline_mode=pl.Buffered(1)),
                  pl.BlockSpec((tm, D_MODEL), lambda s: (0, 0), pipeline_mode=pl.Buffered(1)),
                  pl.BlockSpec((nh, tm, hw), lambda s: (0, nxt(s), 0)),
                  pl.BlockSpec((tm, D_MODEL), lambda s: (nxt(s), 0)),
                  pl.BlockSpec((None, tm, D_PLE), lambda s: (layer, prv(s), 0)),
                  _const_spec((D_MODEL, D_MODEL)), vec, vec,
                  _layer_spec(layer, (D_MODEL, 2 * D_FF)), _layer_spec(layer, (D_FF, D_MODEL)), vec, vec,
                  _layer_spec(layer, (D_MODEL, D_MODEL)), _layer_spec(layer, (D_PLE, D_MODEL))],
        out_specs=pl.BlockSpec((tm, D_MODEL), lambda s: (prv(s), 0)),
        out_shape=jax.ShapeDtypeStruct((T, D_MODEL), F32),
        scratch_shapes=[pltpu.VMEM((tm, D_MODEL), F32),
                        pltpu.VMEM((tm, D_MODEL), BF16),
                        pltpu.VMEM((tm, D_MODEL), F32)],
        compiler_params=_cparams(("arbitrary",)),
        name="layer_tail",
    )(o, x2, o, x2, p3, w_o.astype(BF16), g1, b1, w_in, w_down, g2, b2, w_gate, w_proj)


def _hgrn_pre_kernel(x_ref, w_ref, lbl_ref, q_out, k_out, lf_out, v_out, sg_out, *, layer):
    n = HGRN_HEADS * HGRN_DK
    logits = lbl_ref[...]
    mx = jnp.max(logits, axis=0, keepdims=True)
    e = jnp.exp(logits - mx)
    soft = e / jnp.sum(e, axis=0, keepdims=True)
    lb = jnp.zeros((1, n), F32)
    for j in range(1, layer + 1):
        lb = lb + soft[j:j + 1, :]
    log_lb = jnp.log(lb)

    xb = x_ref[...].astype(BF16)
    blocks = [slice(lo_c, lo_c + HGRN_PRE_COLS) for lo_c in range(0, n, HGRN_PRE_COLS)]

    def proj(part, cols):
        return _dot(xb, w_ref[:, part * n + cols.start:part * n + cols.stop])

    def put(out_ref, cols, val):
        for lo_c in range(cols.start, cols.stop, HGRN_DK):
            out_ref[lo_c // HGRN_DK] = val[:, lo_c - cols.start:lo_c - cols.start + HGRN_DK].astype(out_ref.dtype)

    for cols in blocks:
        qp = proj(0, cols)
        put(q_out, cols, qp * _sigmoid(qp))
        f = proj(1, cols)
        lbc = lb[:, cols]
        u = jnp.exp(-jnp.abs(f))
        t = 1.0 + u
        r = 1.0 / t
        put(k_out, cols, (1.0 - lbc) * jnp.where(f >= 0, u * r, r))
        num = jnp.where(f >= 0, 1.0 + lbc * u, lbc + u)
        log_num = jnp.maximum(jnp.log(num), jnp.maximum(log_lb[:, cols], jnp.minimum(f, 0.0)))
        put(lf_out, cols, log_num - jnp.log(t))
        put(v_out, cols, proj(2, cols))
        gp = proj(3, cols)
        put(sg_out, cols, gp * _sigmoid(gp))


def _hgrn_pre(x2, w_in, lb_logits, layer):
    T = x2.shape[0]
    tm = min(HGRN_PRE_TILE, T)
    n = HGRN_HEADS * HGRN_DK
    row = lambda i: (i, 0)
    outs = [jax.ShapeDtypeStruct((HGRN_HEADS, T, HGRN_DK), dt) for dt in (F32, F32, F32, BF16, F32)]
    return pl.pallas_call(
        functools.partial(_hgrn_pre_kernel, layer=layer),
        grid=(T // tm,),
        in_specs=[pl.BlockSpec((tm, D_MODEL), row), _const_spec((D_MODEL, 4 * n)),
                  _const_spec((DEPTH, n))],
        out_specs=[pl.BlockSpec((HGRN_HEADS, tm, HGRN_DK), _head_major_rows)] * 5,
        out_shape=outs,
        compiler_params=_cparams(("parallel",)),
        name="hgrn_pre",
    )(x2, w_in.astype(BF16), lb_logits)


def _split3(g):
    g1 = g.astype(BF16)
    r1 = g - g1.astype(F32)
    g2 = r1.astype(BF16)
    g3 = (r1 - g2.astype(F32)).astype(BF16)
    return g1, g2, g3


def _level_operand(h, G, g, q, k, g_sc, rowi):
    chunk = G.shape[0]

    def bc(row):
        return jnp.broadcast_to(g_sc[row:row + 1, :], (SUBLANES, LANES))

    if h >= SUBLANES:
        pieces = []
        for r0 in range(0, chunk, SUBLANES):
            ref = bc((r0 // (2 * h)) * (2 * h) + h - 1)
            rows = slice(r0, r0 + SUBLANES)
            if (r0 // h) % 2 == 1:
                pieces.append(q[rows] * jnp.exp2(G[rows] - ref))
            else:
                pieces.append(k[rows] * jnp.exp2(ref - G[rows]))
        return jnp.concatenate(pieces, axis=0)

    up = (rowi & h) != 0
    if h == 1:
        arg = jnp.where(up, g, 0.0)
    else:
        sub = lax.broadcasted_iota(jnp.int32, (SUBLANES, LANES), 0)
        pieces = []
        for r0 in range(0, chunk, SUBLANES):
            ref = bc(r0 + h - 1)
            for start in range(2 * h, SUBLANES, 2 * h):
                ref = jnp.where(sub >= start, bc(r0 + start + h - 1), ref)
            pieces.append(ref)
        d = G - jnp.concatenate(pieces, axis=0)
        arg = jnp.where(up, d, -d)
    return jnp.where(up, q, k) * jnp.exp2(arg)


def _hgrn_rec_kernel(q_ref, k_ref, lf_ref, v_ref, sg_ref, gn_ref, o_ref,
                     st_sc, g_sc, a_sc, qg_sc, kd_sc, *, chunk, n_chunks):
    @pl.when(pl.program_id(2) == 0)
    def _():
        st_sc[...] = jnp.zeros(st_sc.shape, F32)

    ti = lax.broadcasted_iota(jnp.int32, (chunk, chunk), 0)
    si = lax.broadcasted_iota(jnp.int32, (chunk, chunk), 1)
    tri = (si <= ti).astype(BF16)
    lvl = jnp.where(ti >= si, 31 - lax.clz(jnp.bitwise_xor(ti, si)), -2)
    rowi = lax.broadcasted_iota(jnp.int32, (chunk, LANES), 0)
    gn = gn_ref[...]

    def rows_of(c):
        return slice(c * chunk, (c + 1) * chunk)

    def cumulate(c):
        rows = rows_of(c)
        g1, g2, g3 = _split3(lf_ref[0, rows, :] * LOG2_E)
        g_sc[rows, :] = _dot(tri, g1) + _dot(tri, g2) + _dot(tri, g3)

    def decay_matrix(c):
        rows = rows_of(c)
        g = lf_ref[0, rows, :] * LOG2_E
        G = g_sc[rows, :]
        q = q_ref[0, rows, :]
        k = k_ref[0, rows, :]
        g_last = G[chunk - 1:chunk, :]
        qg_sc[rows, :] = (q * jnp.exp2(G)).astype(BF16)
        kd_sc[rows, :] = (k * jnp.exp2(g_last - G)).astype(BF16)
        a = jnp.where(lvl == -1, jnp.sum(q * k, axis=1, keepdims=True), 0.0)
        h = chunk // 2
        while h >= 1:
            z32 = _level_operand(h, G, g, q, k, g_sc.at[rows, :], rowi)
            z = z32.astype(BF16)
            level = h.bit_length() - 1
            if h >= SUBLANES:
                groups = range(0, chunk, SUBLANES)
                upper = [r0 for r0 in groups if (r0 // h) % 2 == 1]
                zu = jnp.concatenate([z32[r0:r0 + SUBLANES] for r0 in upper], axis=0).astype(BF16)
                al = _dot_nt(zu, z)
                pieces = []
                for r0 in groups:
                    grp = slice(r0, r0 + SUBLANES)
                    if r0 in upper:
                        u0 = upper.index(r0) * SUBLANES
                        pieces.append(jnp.where(lvl[grp] == level, al[u0:u0 + SUBLANES], a[grp]))
                    else:
                        pieces.append(a[grp])
                a = jnp.concatenate(pieces, axis=0)
            else:
                a = jnp.where(lvl == level, _dot_nt(z, z), a)
            h //= 2
        a_sc[rows, :] = a.astype(BF16)

    def advance(c, st):
        rows = rows_of(c)
        v = v_ref[0, rows, :]
        o = _dot_nt(qg_sc[rows, :], st.astype(BF16)) + _dot(a_sc[rows, :], v)
        upd = _dot(v.astype(F32).T.astype(BF16), kd_sc[rows, :])
        st = st * jnp.exp2(g_sc[(c + 1) * chunk - 1:(c + 1) * chunk, :]) + upd
        on = _rms_norm(o, gn)
        o_ref[0, rows, :] = (on * sg_ref[0, rows, :]).astype(o_ref.dtype)
        return st

    st = st_sc[...]
    for step in range(n_chunks + 2):
        if step < n_chunks:
            cumulate(step)
        if 1 <= step <= n_chunks:
            decay_matrix(step - 1)
        if step >= 2:
            st = advance(step - 2, st)
    st_sc[...] = st


def _hgrn_rec(q, k, lf, v, sg, out_norm, B, S):
    nh = HGRN_HEADS
    ts = min(HGRN_SEQ_TILE, S)
    chunk = min(HGRN_CHUNK, ts)
    r4 = lambda t: t.reshape(nh, B, S, HGRN_DK)
    blk = pl.BlockSpec((None, 1, ts, LANES), lambda b, h, i: (h, b, i, 0))
    out = pl.pallas_call(
        functools.partial(_hgrn_rec_kernel, chunk=chunk, n_chunks=ts // chunk),
        grid=(B, nh, S // ts),
        in_specs=[blk, blk, blk, blk, blk, pl.BlockSpec((1, LANES), lambda b, h, i: (0, h))],
        out_specs=blk,
        out_shape=jax.ShapeDtypeStruct((nh, B, S, HGRN_DV), BF16),
        scratch_shapes=[pltpu.VMEM((HGRN_DV, HGRN_DK), F32),
                        pltpu.VMEM((ts, LANES), F32),
                        pltpu.VMEM((ts, chunk), BF16),
                        pltpu.VMEM((ts, LANES), BF16),
                        pltpu.VMEM((ts, LANES), BF16)],
        compiler_params=_cparams(("parallel", "parallel", "arbitrary")),
        name="hgrn_rec",
    )(r4(q), r4(k), r4(lf), r4(v), r4(sg), out_norm[None, :])
    return out.reshape(nh, B * S, HGRN_DV)


def kernel(x, p, positions, mla_w_dqkv, mla_q_norm, mla_kv_norm, mla_w_uq, mla_w_ukv, mla_w_o,
           hgrn_w_in, hgrn_lb_logits, hgrn_out_norm, hgrn_w_o, ffn_w_in, ffn_w_down,
           ln_mix_g, ln_mix_b, ln_ffn_g, ln_ffn_b, ple_w_proj, ple_w_gate):
    B, S, D = x.shape
    T = B * S
    x2 = x.reshape(T, D)
    pos2 = positions.astype(F32).reshape(1, T)
    depth = p.shape[0]
    p3 = p.reshape(depth, T, D_PLE)
    vecs = [v.reshape(depth, 1, D_MODEL) for v in (ln_mix_g, ln_mix_b, ln_ffn_g, ln_ffn_b)]
    ffn_in, ffn_down, w_gate, w_proj = (w.astype(BF16) for w in (ffn_w_in, ffn_w_down, ple_w_gate, ple_w_proj))
    for i in range(depth):
        j = i // N_MIXERS
        if i % N_MIXERS == 0:
            q, k, kr, v = _mla_pre(x2, pos2, mla_w_dqkv[j], mla_q_norm[j], mla_kv_norm[j],
                                   mla_w_uq[j], mla_w_ukv[j])
            o = _flash_attn(q, k, kr, v, B, S)
            w_o = mla_w_o[j]
        else:
            q, k, lf, v, sg = _hgrn_pre(x2, hgrn_w_in[j], hgrn_lb_logits, i)
            o = _hgrn_rec(q, k, lf, v, sg, hgrn_out_norm[j], B, S)
            w_o = hgrn_w_o[j]
        x2 = _layer_tail(i, o, x2, p3, w_o, vecs[0], vecs[1], ffn_in, ffn_down, vecs[2], vecs[3],
                         w_gate, w_proj)
    return x2.reshape(B, S, D)
```
